```python
import jax, jax.numpy as jnp
from jax import lax
import numpy as np

D_MODEL = 2048
BATCH = 1
SEQ = 8192
DEPTH = 2

PLE_DIM = 256
NORM_EPS = 1e-6
CONV_WIDTH = 4
N_BRANCH = 3
MIX_WIDTH = D_MODEL // 2
D_FF = 4 * D_MODEL

DN_HEAD_DIM = 128
DN_HEADS = MIX_WIDTH // DN_HEAD_DIM
DN_CHUNK = 64

SSM_HEAD_DIM = 64
SSM_HEADS = MIX_WIDTH // SSM_HEAD_DIM
SSM_GROUPS = 2
SSM_STATE = 128
SSM_CHUNK = 64

GLA_HEADS = 4
GLA_K_WIDTH = MIX_WIDTH // 2
GLA_K_DIM = GLA_K_WIDTH // GLA_HEADS
GLA_V_DIM = MIX_WIDTH // GLA_HEADS
GLA_GATE_RANK = 16
GLA_GATE_TEMP = 16.0
GLA_CHUNK = 16

IN_SPLITS = (
    MIX_WIDTH, MIX_WIDTH, MIX_WIDTH, DN_HEADS, DN_HEADS, MIX_WIDTH,
    MIX_WIDTH, MIX_WIDTH, SSM_GROUPS * SSM_STATE, SSM_GROUPS * SSM_STATE, SSM_HEADS,
    GLA_K_WIDTH, GLA_K_WIDTH, MIX_WIDTH, GLA_GATE_RANK, MIX_WIDTH,
    N_BRANCH * D_MODEL,
)
IN_TOTAL = sum(IN_SPLITS)

kernel_name = "hybrid_deltanet_ssd_gla_block"


def rmsnorm(x, gain):
    xf = x.astype(jnp.float32)
    y = xf * lax.rsqrt(jnp.mean(xf * xf, axis=-1, keepdims=True) + NORM_EPS)
    return (y * gain.astype(jnp.float32)).astype(x.dtype)


def l2norm(x):
    xf = x.astype(jnp.float32)
    return xf * lax.rsqrt(jnp.sum(xf * xf, axis=-1, keepdims=True) + NORM_EPS)


def causal_depthwise_conv(x, w, b=None):
    K, C = w.shape
    y = lax.conv_general_dilated(x, w[:, None, :].astype(x.dtype), window_strides=(1,),
                                 padding=[(K - 1, 0)], dimension_numbers=('NWC', 'WIO', 'NWC'),
                                 feature_group_count=C)
    if b is not None:
        y = y + b.astype(y.dtype)
    return y


def _to_chunks(t, chunk):
    Bsz, T, H = t.shape[:3]
    t = t.reshape((Bsz, T // chunk, chunk, H) + t.shape[3:])
    return jnp.moveaxis(t, 2, 3)


def _from_chunks(t):
    N, Bsz, H, C, V = t.shape
    return t.transpose(1, 0, 3, 2, 4).reshape(Bsz, N * C, H, V)


def chunk_gated_delta_rule(q, k, v, g, beta, chunk=DN_CHUNK):
    Bsz, T, H, Kd = q.shape
    Vd = v.shape[-1]
    q = _to_chunks(q * (Kd ** -0.5), chunk)
    k = _to_chunks(k, chunk)
    v = _to_chunks(v, chunk)
    beta = _to_chunks(beta, chunk)
    g = jnp.cumsum(_to_chunks(g, chunk), axis=-1)
    idx = jnp.arange(chunk)
    causal = idx[:, None] >= idx[None, :]
    strict = idx[:, None] > idx[None, :]
    decay = jnp.exp(jnp.where(causal, g[..., :, None] - g[..., None, :], -jnp.inf))
    kb = k * beta[..., None]
    m = jnp.where(strict, jnp.einsum('bnhik,bnhjk->bnhij', kb, k) * decay, 0.0)
    a = m + jnp.eye(chunk, dtype=m.dtype)
    u = lax.linalg.triangular_solve(a, v * beta[..., None], left_side=True, lower=True, unit_diagonal=True)
    w = lax.linalg.triangular_solve(a, kb * jnp.exp(g)[..., None], left_side=True, lower=True, unit_diagonal=True)
    attn = jnp.einsum('bnhik,bnhjk->bnhij', q, k) * decay
    qg = q * jnp.exp(g)[..., None]
    kd = k * jnp.exp(g[..., -1:] - g)[..., None]
    g_last = jnp.exp(g[..., -1])

    def step(S, xs):
        qg_c, kd_c, u_c, w_c, attn_c, gl_c = xs
        v_new = u_c - jnp.einsum('bhck,bhkv->bhcv', w_c, S)
        o = jnp.einsum('bhck,bhkv->bhcv', qg_c, S) + jnp.einsum('bhij,bhjv->bhiv', attn_c, v_new)
        S = S * gl_c[..., None, None] + jnp.einsum('bhck,bhcv->bhkv', kd_c, v_new)
        return S, o

    xs = tuple(jnp.moveaxis(t, 1, 0) for t in (qg, kd, u, w, attn, g_last))
    S0 = jnp.zeros((Bsz, H, Kd, Vd), jnp.float32)
    _, o = lax.scan(step, S0, xs)
    return _from_chunks(o)


def ssd_chunked(xdt, a, Bm, Cm, chunk=SSM_CHUNK):
    Bsz, T, H, P = xdt.shape
    G, N = Bm.shape[2:]
    R = H // G
    Nc = T // chunk
    xdt = xdt.reshape(Bsz, Nc, chunk, G, R, P)
    a = a.reshape(Bsz, Nc, chunk, G, R).transpose(0, 1, 3, 4, 2)
    Bm = Bm.reshape(Bsz, Nc, chunk, G, N)
    Cm = Cm.reshape(Bsz, Nc, chunk, G, N)
    acs = jnp.cumsum(a, axis=-1)
    idx = jnp.arange(chunk)
    causal = idx[:, None] >= idx[None, :]
    lmat = jnp.exp(jnp.where(causal, acs[..., :, None] - acs[..., None, :], -jnp.inf))
    cb = jnp.einsum('bclgn,bcsgn->bcgls', Cm, Bm)
    y_diag = jnp.einsum('bcgls,bcgrls,bcsgrp->bclgrp', cb, lmat, xdt)
    states = jnp.einsum('bclgn,bcgrl,bclgrp->bcgrpn', Bm, jnp.exp(acs[..., -1:] - acs), xdt)
    chunk_decay = jnp.exp(acs[..., -1])

    def step(S, inp):
        st, dec = inp
        return S * dec[..., None, None] + st, S

    S0 = jnp.zeros((Bsz, G, R, P, N), jnp.float32)
    _, s_in = lax.scan(step, S0, (jnp.moveaxis(states, 1, 0), jnp.moveaxis(chunk_decay, 1, 0)))
    s_in = jnp.moveaxis(s_in, 0, 1)
    y_off = jnp.einsum('bclgn,bcgrpn,bcgrl->bclgrp', Cm, s_in, jnp.exp(acs))
    return (y_diag + y_off).reshape(Bsz, T, H, P)


def chunk_gla(q, k, v, gk, chunk=GLA_CHUNK):
    Bsz, T, H, Kd = q.shape
    Vd = v.shape[-1]
    q = _to_chunks(q * (Kd ** -0.5), chunk)
    k = _to_chunks(k, chunk)
    v = _to_chunks(v, chunk)
    G = jnp.cumsum(_to_chunks(gk, chunk), axis=3)
    idx = jnp.arange(chunk)
    causal = (idx[:, None] >= idx[None, :])[:, :, None]
    pair = jnp.exp(jnp.where(causal, G[..., :, None, :] - G[..., None, :, :], -jnp.inf))
    attn = jnp.einsum('bnhik,bnhjk,bnhijk->bnhij', q, k, pair)
    qg = q * jnp.exp(G)
    kd = k * jnp.exp(G[..., -1:, :] - G)
    dec = jnp.exp(G[..., -1, :])

    def step(S, xs):
        qg_c, kd_c, v_c, attn_c, dec_c = xs
        o = jnp.einsum('bhck,bhkv->bhcv', qg_c, S) + jnp.einsum('bhij,bhjv->bhiv', attn_c, v_c)
        S = S * dec_c[..., None] + jnp.einsum('bhck,bhcv->bhkv', kd_c, v_c)
        return S, o

    xs = tuple(jnp.moveaxis(t, 1, 0) for t in (qg, kd, v, attn, dec))
    S0 = jnp.zeros((Bsz, H, Kd, Vd), jnp.float32)
    _, o = lax.scan(step, S0, xs)
    return _from_chunks(o)


def gated_deltanet_branch(q, k, v, beta_logit, a_logit, gate, conv_w, a_log, dt_bias, norm_w):
    dtype = q.dtype
    Bsz, T, _ = q.shape
    qkv = jax.nn.silu(causal_depthwise_conv(jnp.concatenate([q, k, v], axis=-1), conv_w))
    q, k, v = jnp.split(qkv, 3, axis=-1)
    q = l2norm(q.reshape(Bsz, T, DN_HEADS, DN_HEAD_DIM))
    k = l2norm(k.reshape(Bsz, T, DN_HEADS, DN_HEAD_DIM))
    v = v.reshape(Bsz, T, DN_HEADS, DN_HEAD_DIM).astype(jnp.float32)
    beta = jax.nn.sigmoid(beta_logit.astype(jnp.float32))
    g = -jnp.exp(a_log.astype(jnp.float32)) * jax.nn.softplus(a_logit.astype(jnp.float32) + dt_bias.astype(jnp.float32))
    o = chunk_gated_delta_rule(q, k, v, g, beta)
    o = rmsnorm(o, norm_w) * jax.nn.silu(gate.reshape(Bsz, T, DN_HEADS, DN_HEAD_DIM).astype(jnp.float32))
    return o.reshape(Bsz, T, MIX_WIDTH).astype(dtype)


def mamba2_branch(z, xs, Bm, Cm, dt_raw, conv_w, conv_b, dt_bias, a_log, d_skip, norm_w):
    dtype = xs.dtype
    Bsz, T, _ = xs.shape
    xbc = jax.nn.silu(causal_depthwise_conv(jnp.concatenate([xs, Bm, Cm], axis=-1), conv_w, conv_b))
    xs, Bm, Cm = jnp.split(xbc, [MIX_WIDTH, MIX_WIDTH + SSM_GROUPS * SSM_STATE], axis=-1)
    x = xs.reshape(Bsz, T, SSM_HEADS, SSM_HEAD_DIM).astype(jnp.float32)
    Bm = Bm.reshape(Bsz, T, SSM_GROUPS, SSM_STATE).astype(jnp.float32)
    Cm = Cm.reshape(Bsz, T, SSM_GROUPS, SSM_STATE).astype(jnp.float32)
    dt = jax.nn.softplus(dt_raw.astype(jnp.float32) + dt_bias.astype(jnp.float32))
    A = -jnp.exp(a_log.astype(jnp.float32))
    y = ssd_chunked(x * dt[..., None], dt * A, Bm, Cm)
    y = y + d_skip.astype(jnp.float32)[:, None] * x
    y = y.reshape(Bsz, T, MIX_WIDTH) * jax.nn.silu(z.astype(jnp.float32))
    y = rmsnorm(y.reshape(Bsz, T, SSM_GROUPS, MIX_WIDTH // SSM_GROUPS), norm_w.reshape(SSM_GROUPS, -1))
    return y.reshape(Bsz, T, MIX_WIDTH).astype(dtype)


def gla_branch(q, k, v, gate_lr, out_gate, w2, b2, norm_w):
    dtype = q.dtype
    Bsz, T, _ = q.shape
    q = q.reshape(Bsz, T, GLA_HEADS, GLA_K_DIM).astype(jnp.float32)
    k = k.reshape(Bsz, T, GLA_HEADS, GLA_K_DIM).astype(jnp.float32)
    v = v.reshape(Bsz, T, GLA_HEADS, GLA_V_DIM).astype(jnp.float32)
    gk = jax.nn.log_sigmoid(jnp.einsum('btr,rk->btk', gate_lr.astype(jnp.float32), w2.astype(jnp.float32))
                            + b2.astype(jnp.float32)) / GLA_GATE_TEMP
    gk = gk.reshape(Bsz, T, GLA_HEADS, GLA_K_DIM)
    o = chunk_gla(q, k, v, gk)
    o = rmsnorm(o, norm_w) * jax.nn.silu(out_gate.reshape(Bsz, T, GLA_HEADS, GLA_V_DIM).astype(jnp.float32))
    return o.reshape(Bsz, T, MIX_WIDTH).astype(dtype)


def setup_inputs(seed: int = 0) -> dict:
    key = jax.random.key(seed)
    ks = iter(jax.random.split(key, 40))

    def nrm(shape, scale):
        return jax.random.normal(next(ks), shape, jnp.float32) * scale

    def gain(shape):
        return 1.0 + nrm(shape, 0.02)

    def log_a(n):
        return jnp.log(jax.random.uniform(next(ks), (DEPTH, n), jnp.float32, 1.0, 16.0))

    def dt_bias(n):
        dt = jnp.exp(jax.random.uniform(next(ks), (DEPTH, n), jnp.float32, np.log(1e-3), np.log(1e-1)))
        return jnp.log(jnp.expm1(dt))

    return {
        "x": nrm((BATCH, SEQ, D_MODEL), 1.0),
        "p": nrm((DEPTH, BATCH, SEQ, PLE_DIM), 1.0),
        "pre_mix_norm": gain((DEPTH, D_MODEL)),
        "w_in": nrm((DEPTH, D_MODEL, IN_TOTAL), D_MODEL ** -0.5),
        "dn_conv_w": nrm((DEPTH, CONV_WIDTH, 3 * MIX_WIDTH), CONV_WIDTH ** -0.5),
        "dn_a_log": log_a(DN_HEADS),
        "dn_dt_bias": dt_bias(DN_HEADS),
        "dn_norm": gain((DEPTH, DN_HEAD_DIM)),
        "ssm_conv_w": nrm((DEPTH, CONV_WIDTH, MIX_WIDTH + 2 * SSM_GROUPS * SSM_STATE), CONV_WIDTH ** -0.5),
        "ssm_conv_b": nrm((DEPTH, MIX_WIDTH + 2 * SSM_GROUPS * SSM_STATE), 0.02),
        "ssm_dt_bias": dt_bias(SSM_HEADS),
        "ssm_a_log": log_a(SSM_HEADS),
        "ssm_d": gain((DEPTH, SSM_HEADS)),
        "ssm_norm": gain((DEPTH, MIX_WIDTH)),
        "gla_gate_w2": nrm((DEPTH, GLA_GATE_RANK, GLA_K_WIDTH), GLA_GATE_RANK ** -0.5),
        "gla_gate_b": nrm((DEPTH, GLA_K_WIDTH), 0.1),
        "gla_norm": gain((DEPTH, GLA_V_DIM)),
        "w_branch": nrm((DEPTH, N_BRANCH, MIX_WIDTH, D_MODEL), MIX_WIDTH ** -0.5),
        "w_out": nrm((DEPTH, D_MODEL, D_MODEL), D_MODEL ** -0.5),
        "post_mix_norm": gain((DEPTH, D_MODEL)),
        "pre_mlp_norm": gain((DEPTH, D_MODEL)),
        "w_up": nrm((DEPTH, D_MODEL, D_FF), D_MODEL ** -0.5),
        "w_down": nrm((DEPTH, D_FF, D_MODEL), D_FF ** -0.5),
        "post_mlp_norm": gain((DEPTH, D_MODEL)),
        "ple_pre_norm": gain((DEPTH, D_MODEL)),
        "w_ple_gate": nrm((DEPTH, D_MODEL, D_MODEL), D_MODEL ** -0.5),
        "w_ple_proj": nrm((DEPTH, PLE_DIM, D_MODEL), PLE_DIM ** -0.5),
        "ple_post_norm": gain((DEPTH, D_MODEL)),
    }


def reference(x, p, pre_mix_norm, w_in, dn_conv_w, dn_a_log, dn_dt_bias, dn_norm,
              ssm_conv_w, ssm_conv_b, ssm_dt_bias, ssm_a_log, ssm_d, ssm_norm,
              gla_gate_w2, gla_gate_b, gla_norm, w_branch, w_out, post_mix_norm,
              pre_mlp_norm, w_up, w_down, post_mlp_norm,
              ple_pre_norm, w_ple_gate, w_ple_proj, ple_post_norm):
    Bsz, T, D = x.shape
    split_idx = np.cumsum(IN_SPLITS)[:-1].tolist()
    for i in range(DEPTH):
        h = rmsnorm(x, pre_mix_norm[i])
        (dn_q, dn_k, dn_v, dn_b, dn_a, dn_g,
         s_z, s_x, s_B, s_C, s_dt,
         g_q, g_k, g_v, g_lr, g_o, br_gate) = jnp.split(h @ w_in[i], split_idx, axis=-1)
        y_dn = gated_deltanet_branch(dn_q, dn_k, dn_v, dn_b, dn_a, dn_g,
                                     dn_conv_w[i], dn_a_log[i], dn_dt_bias[i], dn_norm[i])
        y_ssm = mamba2_branch(s_z, s_x, s_B, s_C, s_dt, ssm_conv_w[i], ssm_conv_b[i],
                              ssm_dt_bias[i], ssm_a_log[i], ssm_d[i], ssm_norm[i])
        y_gla = gla_branch(g_q, g_k, g_v, g_lr, g_o, gla_gate_w2[i], gla_gate_b[i], gla_norm[i])
        branches = jnp.stack([y_dn, y_ssm, y_gla], axis=2)
        up = jnp.einsum('btnm,nmd->btnd', branches, w_branch[i])
        gates = jax.nn.sigmoid(br_gate.reshape(Bsz, T, N_BRANCH, D))
        mixed = jnp.sum(gates * up, axis=2) @ w_out[i]
        x = x + rmsnorm(mixed, post_mix_norm[i])
        h = rmsnorm(x, pre_mlp_norm[i])
        m = jnp.square(jax.nn.relu(h @ w_up[i])) @ w_down[i]
        x = x + rmsnorm(m, post_mlp_norm[i])
        ple_gate = jax.nn.sigmoid(rmsnorm(x, ple_pre_norm[i]) @ w_ple_gate[i])
        e = (p[i] @ w_ple_proj[i]) * ple_gate
        x = x + rmsnorm(e, ple_post_norm[i])
    return x
```

```python
import functools

import numpy as np
import jax
import jax.numpy as jnp
from jax import lax
from jax.experimental import pallas as pl
from jax.experimental.pallas import tpu as pltpu

F32 = jnp.float32
BF16 = jnp.bfloat16
HIGHEST = lax.Precision.HIGHEST

D_MODEL = 2048
PLE_DIM = 256
NORM_EPS = 1e-6
CONV_WIDTH = 4
N_BRANCH = 3
MIX_WIDTH = D_MODEL // 2
D_FF = 4 * D_MODEL

DN_HEAD_DIM = 128
DN_HEADS = MIX_WIDTH // DN_HEAD_DIM
SSM_HEAD_DIM = 64
SSM_HEADS = MIX_WIDTH // SSM_HEAD_DIM
SSM_GROUPS = 2
SSM_STATE = 128
GLA_HEADS = 4
GLA_K_WIDTH = MIX_WIDTH // 2
GLA_K_DIM = GLA_K_WIDTH // GLA_HEADS
GLA_V_DIM = MIX_WIDTH // GLA_HEADS
GLA_GATE_RANK = 16
GLA_GATE_TEMP = 16.0

CHUNK = 64
LANES = 128
HALO = 8

IN_SPLITS = (
    MIX_WIDTH, MIX_WIDTH, MIX_WIDTH, DN_HEADS, DN_HEADS, MIX_WIDTH,
    MIX_WIDTH, MIX_WIDTH, SSM_GROUPS * SSM_STATE, SSM_GROUPS * SSM_STATE, SSM_HEADS,
    GLA_K_WIDTH, GLA_K_WIDTH, MIX_WIDTH, GLA_GATE_RANK, MIX_WIDTH,
    N_BRANCH * D_MODEL,
)
(_DN_Q, _DN_K, _DN_V, _DN_B, _DN_A, _DN_G, _S_Z, _S_X, _S_B, _S_C, _S_DT,
 _G_Q, _G_K, _G_V, _G_LR, _G_O, _BR) = range(17)
_IN_OFF = np.concatenate([[0], np.cumsum(IN_SPLITS)]).tolist()

_WIDE_ORDER = (_DN_Q, _DN_K, _DN_V, _DN_G, _S_Z, _S_X, _G_V, _G_O, _S_B, _S_C, _G_Q, _G_K, _BR)
_SMALL_ORDER = (_DN_B, _DN_A, _S_DT, _G_LR)
_COL = {}
_off = 0
for _s in _WIDE_ORDER:
    _COL[_s] = _off
    _off += IN_SPLITS[_s]
_SMALL_COL = _off
_SMALL_LANE = {}
_l = 0
for _s in _SMALL_ORDER:
    _SMALL_LANE[_s] = _l
    _l += IN_SPLITS[_s]
IN_PAD = 16384

VMEM_LIMIT = 50 * 1024 * 1024


def _cparams(sem):
    return pltpu.CompilerParams(dimension_semantics=sem, vmem_limit_bytes=VMEM_LIMIT)


def _mm(a, b):
    return jnp.dot(a.astype(BF16), b.astype(BF16), preferred_element_type=F32)


def _mm_nt(a, b):
    return lax.dot_general(a.astype(BF16), b.astype(BF16), (((1,), (1,)), ((), ())),
                           preferred_element_type=F32)


def _mm_tn(a, b):
    return lax.dot_general(a.astype(BF16), b.astype(BF16), (((0,), (0,)), ((), ())),
                           preferred_element_type=F32)


def _mmh(a, b):
    return jnp.dot(a, b, precision=HIGHEST, preferred_element_type=F32)


def _mmh_nt(a, b):
    return lax.dot_general(a, b, (((1,), (1,)), ((), ())), precision=HIGHEST,
                           preferred_element_type=F32)


def _sigmoid(x):
    return 1.0 / (1.0 + jnp.exp(-x))


def _silu(x):
    return x * _sigmoid(x)


def _softplus(x):
    return jnp.maximum(x, 0.0) + jnp.log(1.0 + jnp.exp(-jnp.abs(x)))


def _rms(x, gain):
    return x * lax.rsqrt(jnp.mean(x * x, axis=-1, keepdims=True) + NORM_EPS) * gain


def _iota(shape, dim):
    return lax.broadcasted_iota(jnp.int32, shape, dim)


def _tril_ones(n):
    return (_iota((n, n), 0) >= _iota((n, n), 1)).astype(F32)


def _conv_silu(x_ref, halo_ref, w, bias):
    tb = x_ref.shape[0]
    x = x_ref[...]
    halo_ref[HALO:HALO + tb, :] = x
    y = x * w[3:4, :]
    for k in range(CONV_WIDTH - 1):
        y = y + halo_ref[HALO - 3 + k:HALO - 3 + k + tb, :] * w[k:k + 1, :]
    halo_ref[0:HALO, :] = halo_ref[tb:tb + HALO, :]
    if bias is not None:
        y = y + bias
    return _silu(y)


def _inproj_kernel(x_ref, g_ref, w_ref, o_ref, h_ref):
    @pl.when(pl.program_id(1) == 0)
    def _():
        h_ref[...] = _rms(x_ref[...], g_ref[...]).astype(BF16)

    o_ref[...] = jnp.dot(h_ref[...], w_ref[...], preferred_element_type=F32)


def _inproj(x, gain, w, *, tm=1024, tn=1024):
    T, D = x.shape
    N = w.shape[1]
    return pl.pallas_call(
        _inproj_kernel,
        grid=(T // tm, N // tn),
        in_specs=[pl.BlockSpec((tm, D), lambda i, j: (i, 0)),
                  pl.BlockSpec((1, D), lambda i, j: (0, 0)),
                  pl.BlockSpec((D, tn), lambda i, j: (0, j))],
        out_specs=pl.BlockSpec((tm, tn), lambda i, j: (i, j)),
        out_shape=jax.ShapeDtypeStruct((T, N), F32),
        scratch_shapes=[pltpu.VMEM((tm, D), BF16)],
        compiler_params=_cparams(("parallel", "arbitrary")),
        name="inproj",
    )(x, gain, w)


def _unit_lower_inverse(m):
    n = m.shape[0]
    r = _iota((n, n), 0)
    c = _iota((n, n), 1)
    same = (r // 16) == (c // 16)
    eye = (r == c).astype(F32)
    nd = jnp.where(same, -m, 0.0)
    off = jnp.where(same, 0.0, m)
    z = eye + nd
    p = _mm(nd, nd)
    for _ in range(2):
        zp = _mm(jnp.concatenate([z, p], axis=0), p)
        z = z + zp[:n]
        p = zp[n:]
    xd = z + _mm(z, p)
    n2 = -_mm(xd, off)
    z = eye + n2
    z = z + _mm(z, _mm(n2, n2))
    return _mm(z, xd)


def _dn_kernel(q_ref, k_ref, v_ref, gate_ref, sm_ref, cw_ref, par_ref, nw_ref, o_ref,
               hq_ref, hk_ref, hv_ref, s_ref):
    C = CHUNK

    @pl.when(pl.program_id(0) == 0)
    def _():
        hq_ref[...] = jnp.zeros_like(hq_ref)
        hk_ref[...] = jnp.zeros_like(hk_ref)
        hv_ref[...] = jnp.zeros_like(hv_ref)
        s_ref[...] = jnp.zeros_like(s_ref)

    cw = cw_ref[...]
    qc = _conv_silu(q_ref, hq_ref, cw[:, 0:MIX_WIDTH], None)
    kc = _conv_silu(k_ref, hk_ref, cw[:, MIX_WIDTH:2 * MIX_WIDTH], None)
    vc = _conv_silu(v_ref, hv_ref, cw[:, 2 * MIX_WIDTH:3 * MIX_WIDTH], None)

    sm = sm_ref[...]
    lane = _iota((1, LANES), 1)
    a_lanes = (lane >= _SMALL_LANE[_DN_A]) & (lane < _SMALL_LANE[_DN_A] + DN_HEADS)
    a_neg = jnp.where(a_lanes, -jnp.exp(par_ref[0:1, :]), 0.0)
    beta_t = _sigmoid(sm)
    g_t = a_neg * _softplus(sm + par_ref[1:2, :])
    gc_t = _mmh(_tril_ones(C), g_t)
    eye_l = (_iota((LANES, LANES), 0) == _iota((LANES, LANES), 1)).astype(F32)
    gc_rows = _mmh_nt(eye_l, gc_t)

    r = _iota((C, C), 0)
    c = _iota((C, C), 1)
    causal = r >= c
    strict = r > c
    scale = DN_HEAD_DIM ** -0.5
    nw = nw_ref[...]

    for h in range(DN_HEADS):
        sl = slice(h * DN_HEAD_DIM, (h + 1) * DN_HEAD_DIM)
        qh = qc[:, sl]
        kh = kc[:, sl]
        vh = vc[:, sl]
        qh = qh * lax.rsqrt(jnp.sum(qh * qh, axis=-1, keepdims=True) + NORM_EPS)
        kh = kh * lax.rsqrt(jnp.sum(kh * kh, axis=-1, keepdims=True) + NORM_EPS)
        bl = _SMALL_LANE[_DN_B] + h
        gl = _SMALL_LANE[_DN_A] + h
        beta = beta_t[:, bl:bl + 1]
        gcol = gc_t[:, gl:gl + 1]
        grow = gc_rows[gl:gl + 1, :]
        glast = gcol[C - 1:C, :]
        decay = jnp.where(causal, jnp.exp(gcol - grow), 0.0)
        eg = jnp.exp(gcol)
        kb = kh * beta
        qs = qh * scale
        kq = _mm_nt(jnp.concatenate([kb, qs], axis=0), kh)
        m = jnp.where(strict, kq[:C] * decay, 0.0)
        attn = kq[C:] * decay
        ainv = _unit_lower_inverse(m)
        uw = _mm(ainv, jnp.concatenate([vh * beta, kb * eg], axis=1))
        u = uw[:, :DN_HEAD_DIM]
        w = uw[:, DN_HEAD_DIM:]
        s = s_ref[h]
        ws = _mm(jnp.concatenate([w, qs * eg], axis=0), s)
        v_new = u - ws[:C]
        o = ws[C:] + _mm(attn, v_new)
        kd = kh * jnp.exp(glast - gcol)
        s_ref[h] = s * jnp.exp(glast) + _mm_tn(kd, v_new)
        gate = gate_ref[:, sl]
        o_ref[:, sl] = (_rms(o, nw) * _silu(gate)).astype(o_ref.dtype)


def _dn_branch(proj, conv_w, a_log, dt_bias, norm_w):
    T = proj.shape[0]
    C = CHUNK
    par = jnp.zeros((8, LANES), F32)
    la = _SMALL_LANE[_DN_A]
    par = par.at[0, la:la + DN_HEADS].set(a_log).at[1, la:la + DN_HEADS].set(dt_bias)
    wide = lambda seg: pl.BlockSpec((C, MIX_WIDTH), lambda i, b=_COL[seg] // MIX_WIDTH: (i, b))
    return pl.pallas_call(
        _dn_kernel,
        grid=(T // C,),
        in_specs=[wide(_DN_Q), wide(_DN_K), wide(_DN_V), wide(_DN_G),
                  pl.BlockSpec((C, LANES), lambda i: (i, _SMALL_COL // LANES)),
                  pl.BlockSpec((CONV_WIDTH, 3 * MIX_WIDTH), lambda i: (0, 0)),
                  pl.BlockSpec((8, LANES), lambda i: (0, 0)),
                  pl.BlockSpec((1, DN_HEAD_DIM), lambda i: (0, 0))],
        out_specs=pl.BlockSpec((C, MIX_WIDTH), lambda i: (i, 0)),
        out_shape=jax.ShapeDtypeStruct((T, MIX_WIDTH), BF16),
        scratch_shapes=[pltpu.VMEM((HALO + C, MIX_WIDTH), F32)] * 3
        + [pltpu.VMEM((DN_HEADS, DN_HEAD_DIM, DN_HEAD_DIM), F32)],
        compiler_params=_cparams(("arbitrary",)),
        name="deltanet",
    )(proj, proj, proj, proj, proj, conv_w, par, norm_w.reshape(1, DN_HEAD_DIM))


def _ssd_kernel(z_ref, x_ref, bc_ref, sm_ref, cw_ref, cb_ref, par_ref, dsk_ref, nw_ref, o_ref,
                hx_ref, hbc_ref, st_ref):
    C = CHUNK
    P = SSM_HEAD_DIM
    GW = MIX_WIDTH // SSM_GROUPS
    BCW = 2 * SSM_GROUPS * SSM_STATE

    @pl.when(pl.program_id(0) == 0)
    def _():
        hx_ref[...] = jnp.zeros_like(hx_ref)
        hbc_ref[...] = jnp.zeros_like(hbc_ref)
        st_ref[...] = jnp.zeros_like(st_ref)

    cw = cw_ref[...]
    cb = cb_ref[...]
    xc = _conv_silu(x_ref, hx_ref, cw[:, :MIX_WIDTH], cb[:, :MIX_WIDTH])
    bcc = _conv_silu(bc_ref, hbc_ref, cw[:, MIX_WIDTH:], cb[:, MIX_WIDTH:])

    sm = sm_ref[...]
    lane = _iota((1, LANES), 1)
    l0 = _SMALL_LANE[_S_DT]
    dt_lanes = (lane >= l0) & (lane < l0 + SSM_HEADS)
    a_neg = jnp.where(dt_lanes, -jnp.exp(par_ref[0:1, :]), 0.0)
    dt_t = jnp.where(dt_lanes, _softplus(sm + par_ref[1:2, :]), 0.0)
    acs_t = _mmh(_tril_ones(C), dt_t * a_neg)
    er = _iota((LANES, MIX_WIDTH), 0)
    ec = _iota((LANES, MIX_WIDTH), 1)
    expand = ((er - l0) == (ec // P)).astype(F32)
    both = _mmh(jnp.concatenate([acs_t, dt_t], axis=0), expand)
    acs = both[:C]
    dt_e = both[C:]
    rr = _iota((C, MIX_WIDTH), 0)
    cc = _iota((C, MIX_WIDTH), 1) % P
    row_acs = jnp.sum(jnp.where(rr == cc, acs, 0.0), axis=0, keepdims=True)
    last = acs[C - 1:C, :]
    lmat = jnp.where(rr >= cc, jnp.exp(acs - row_acs), 0.0)
    xdt = xc * dt_e
    e_acs = jnp.exp(acs)
    xw = xdt * jnp.exp(last - acs)
    e_last = jnp.exp(last)
    lane128 = _iota((C, LANES), 1)

    ys = []
    for g in range(SSM_GROUPS):
        bg = bcc[:, g * SSM_STATE:(g + 1) * SSM_STATE]
        cg = bcc[:, SSM_GROUPS * SSM_STATE + g * SSM_STATE:SSM_GROUPS * SSM_STATE + (g + 1) * SSM_STATE]
        cb2 = _mm_nt(cg, jnp.concatenate([bg, bg], axis=0))
        st = st_ref[g]
        gs = slice(g * GW, (g + 1) * GW)
        y_off = _mm(cg, st) * e_acs[:, gs]
        parts = []
        for pidx in range(GW // LANES):
            col = g * GW + pidx * LANES
            wmat = cb2 * lmat[:, col:col + LANES]
            xp = xdt[:, col:col + LANES]
            bd = jnp.concatenate([jnp.where(lane128 < P, xp, 0.0),
                                  jnp.where(lane128 >= P, xp, 0.0)], axis=0)
            parts.append(_mm(wmat, bd))
        ys.append(jnp.concatenate(parts, axis=1) + y_off)
        st_ref[g] = st * e_last[:, gs] + _mm_tn(bg, xw[:, gs])
    y = jnp.concatenate(ys, axis=1) + dsk_ref[...] * xc
    y = y * _silu(z_ref[...])
    nw = nw_ref[...]
    for g in range(SSM_GROUPS):
        gs = slice(g * GW, (g + 1) * GW)
        o_ref[:, gs] = _rms(y[:, gs], nw[:, gs]).astype(o_ref.dtype)


def _ssd_branch(proj, conv_w, conv_b, dt_bias, a_log, d_skip, norm_w):
    T = proj.shape[0]
    C = CHUNK
    BCW = 2 * SSM_GROUPS * SSM_STATE
    l0 = _SMALL_LANE[_S_DT]
    par = jnp.zeros((8, LANES), F32)
    par = par.at[0, l0:l0 + SSM_HEADS].set(a_log).at[1, l0:l0 + SSM_HEADS].set(dt_bias)
    dsk = jnp.repeat(d_skip, SSM_HEAD_DIM).reshape(1, MIX_WIDTH)
    wide = lambda seg: pl.BlockSpec((C, MIX_WIDTH), lambda i, b=_COL[seg] // MIX_WIDTH: (i, b))
    cw_total = MIX_WIDTH + BCW
    return pl.pallas_call(
        _ssd_kernel,
        grid=(T // C,),
        in_specs=[wide(_S_Z), wide(_S_X),
                  pl.BlockSpec((C, BCW), lambda i: (i, _COL[_S_B] // BCW)),
                  pl.BlockSpec((C, LANES), lambda i: (i, _SMALL_COL // LANES)),
                  pl.BlockSpec((CONV_WIDTH, cw_total), lambda i: (0, 0)),
                  pl.BlockSpec((1, cw_total), lambda i: (0, 0)),
                  pl.BlockSpec((8, LANES), lambda i: (0, 0)),
                  pl.BlockSpec((1, MIX_WIDTH), lambda i: (0, 0)),
                  pl.BlockSpec((1, MIX_WIDTH), lambda i: (0, 0))],
        out_specs=pl.BlockSpec((C, MIX_WIDTH), lambda i: (i, 0)),
        out_shape=jax.ShapeDtypeStruct((T, MIX_WIDTH), BF16),
        scratch_shapes=[pltpu.VMEM((HALO + C, MIX_WIDTH), F32),
                        pltpu.VMEM((HALO + C, BCW), F32),
                        pltpu.VMEM((SSM_GROUPS, SSM_STATE, MIX_WIDTH // SSM_GROUPS), F32)],
        compiler_params=_cparams(("arbitrary",)),
        name="ssd",
    )(proj, proj, proj, proj, conv_w, conv_b.reshape(1, cw_total), par, dsk,
      norm_w.reshape(1, MIX_WIDTH))


_GLA_LEVELS = (32, 16, 8, 4, 2, 1)


def _gla_cumsum_matrix():
    i = np.arange(CHUNK)[:, None]
    t = np.arange(CHUNK)[None, :]
    mats = []
    for b in (CHUNK,) + _GLA_LEVELS:
        same = (i // b) == (t // b)
        mats.append((same & (t <= i)).astype(np.float32))
        mats.append((same & (t > i)).astype(np.float32))
    return np.concatenate(mats, axis=0)


def _gla_kernel(q_ref, k_ref, v_ref, og_ref, sm_ref, w2_ref, b2_ref, cm_ref, nw_ref, o_ref, st_ref):
    C = CHUNK

    @pl.when(pl.program_id(0) == 0)
    def _():
        st_ref[...] = jnp.zeros_like(st_ref)

    x = _mmh(sm_ref[...], w2_ref[...]) + b2_ref[...]
    gk = -_softplus(-x) * (1.0 / GLA_GATE_TEMP)
    cs = _mmh(cm_ref[...], gk)
    g_inc = cs[0:C]
    g_rev = cs[C:2 * C]
    r = _iota((C, C), 0)
    c = _iota((C, C), 1)
    rcol = _iota((C, 1), 0)
    scale = GLA_K_DIM ** -0.5
    nw = nw_ref[...]

    for h in range(GLA_HEADS):
        ks = slice(h * GLA_K_DIM, (h + 1) * GLA_K_DIM)
        vs = slice(h * GLA_V_DIM, (h + 1) * GLA_V_DIM)
        q = q_ref[:, ks] * scale
        k = k_ref[:, ks]
        v = v_ref[:, vs]
        attn = jnp.where(r == c, _mm_nt(q, k), 0.0)
        for lvl, b in enumerate(_GLA_LEVELS):
            base = 2 * C * (lvl + 1)
            pq = cs[base:base + C, ks]
            pk = cs[base + C:base + 2 * C, ks]
            upper = ((rcol // b) % 2) == 1
            qt = jnp.where(upper, q * jnp.exp(pq), 0.0)
            kt = jnp.where(upper, 0.0, k * jnp.exp(pk))
            pair = ((r // (2 * b)) == (c // (2 * b))) & (((r // b) % 2) == 1) & (((c // b) % 2) == 0)
            attn = attn + jnp.where(pair, _mm_nt(qt, kt), 0.0)
        qg = q * jnp.exp(g_inc[:, ks])
        kd = k * jnp.exp(g_rev[:, ks])
        st = st_ref[h]
        o = _mm_nt(qg, st) + _mm(attn, v)
        st_ref[h] = st * jnp.exp(g_inc[C - 1:C, ks]) + _mm_tn(v, kd)
        og = og_ref[:, vs]
        o_ref[:, vs] = (_rms(o, nw) * _silu(og)).astype(o_ref.dtype)


def _gla_branch(proj, w2, b2, norm_w):
    T = proj.shape[0]
    C = CHUNK
    l0 = _SMALL_LANE[_G_LR]
    w2p = jnp.zeros((LANES, GLA_K_WIDTH), F32).at[l0:l0 + GLA_GATE_RANK].set(w2)
    cm = jnp.asarray(_gla_cumsum_matrix())
    wide = lambda seg: pl.BlockSpec((C, MIX_WIDTH), lambda i, b=_COL[seg] // MIX_WIDTH: (i, b))
    half = lambda seg: pl.BlockSpec((C, GLA_K_WIDTH), lambda i, b=_COL[seg] // GLA_K_WIDTH: (i, b))
    return pl.pallas_call(
        _gla_kernel,
        grid=(T // C,),
        in_specs=[half(_G_Q), half(_G_K), wide(_G_V), wide(_G_O),
                  pl.BlockSpec((C, LANES), lambda i: (i, _SMALL_COL // LANES)),
                  pl.BlockSpec((LANES, GLA_K_WIDTH), lambda i: (0, 0)),
                  pl.BlockSpec((1, GLA_K_WIDTH), lambda i: (0, 0)),
                  pl.BlockSpec(cm.shape, lambda i: (0, 0)),
                  pl.BlockSpec((1, GLA_V_DIM), lambda i: (0, 0))],
        out_specs=pl.BlockSpec((C, MIX_WIDTH), lambda i: (i, 0)),
        out_shape=jax.ShapeDtypeStruct((T, MIX_WIDTH), BF16),
        scratch_shapes=[pltpu.VMEM((GLA_HEADS, GLA_V_DIM, GLA_K_DIM), F32)],
        compiler_params=_cparams(("arbitrary",)),
        name="gla",
    )(proj, proj, proj, proj, proj, w2p, b2.reshape(1, GLA_K_WIDTH), cm,
      norm_w.reshape(1, GLA_V_DIM))


def _merge_kernel(ya_ref, yb_ref, yc_ref, ga_ref, gb_ref, gc_ref, w_ref, o_ref):
    acc = _sigmoid(ga_ref[...]) * jnp.dot(ya_ref[...], w_ref[0], preferred_element_type=F32)
    acc += _sigmoid(gb_ref[...]) * jnp.dot(yb_ref[...], w_ref[1], preferred_element_type=F32)
    acc += _sigmoid(gc_ref[...]) * jnp.dot(yc_ref[...], w_ref[2], preferred_element_type=F32)
    o_ref[...] = acc.astype(o_ref.dtype)


def _merge(y_dn, y_ssm, y_gla, proj, w_branch, *, tm=512, tn=512):
    T = proj.shape[0]
    ysp = pl.BlockSpec((tm, MIX_WIDTH), lambda i, j: (i, 0))
    gate = lambda b: pl.BlockSpec(
        (tm, tn), lambda i, j, base=(_COL[_BR] + b * D_MODEL) // tn: (i, base + j))
    return pl.pallas_call(
        _merge_kernel,
        grid=(T // tm, D_MODEL // tn),
        in_specs=[ysp, ysp, ysp, gate(0), gate(1), gate(2),
                  pl.BlockSpec((N_BRANCH, MIX_WIDTH, tn), lambda i, j: (0, 0, j))],
        out_specs=pl.BlockSpec((tm, tn), lambda i, j: (i, j)),
        out_shape=jax.ShapeDtypeStruct((T, D_MODEL), BF16),
        compiler_params=_cparams(("parallel", "arbitrary")),
        name="merge",
    )(y_dn, y_ssm, y_gla, proj, proj, proj, w_branch)


def _outproj_kernel(m_ref, x_ref, w_ref, g_ref, o_ref):
    mixed = jnp.dot(m_ref[...], w_ref[...], preferred_element_type=F32)
    o_ref[...] = x_ref[...] + _rms(mixed, g_ref[...])


def _outproj(mixed_pre, x, w_out, gain, *, tm=512):
    T, D = x.shape
    return pl.pallas_call(
        _outproj_kernel,
        grid=(T // tm,),
        in_specs=[pl.BlockSpec((tm, D), lambda i: (i, 0)),
                  pl.BlockSpec((tm, D), lambda i: (i, 0)),
                  pl.BlockSpec((D, D), lambda i: (0, 0)),
                  pl.BlockSpec((1, D), lambda i: (0, 0))],
        out_specs=pl.BlockSpec((tm, D), lambda i: (i, 0)),
        out_shape=jax.ShapeDtypeStruct((T, D), F32),
        compiler_params=_cparams(("parallel",)),
        name="outproj",
    )(mixed_pre, x, w_out, gain)


def _mlp_kernel(x_ref, gpre_ref, wu_ref, wd_ref, gpost_ref, o_ref, h_ref, acc_ref):
    f = pl.program_id(1)

    @pl.when(f == 0)
    def _():
        h_ref[...] = _rms(x_ref[...], gpre_ref[...]).astype(BF16)
        acc_ref[...] = jnp.zeros_like(acc_ref)

    up = jnp.dot(h_ref[...], wu_ref[...], preferred_element_type=F32)
    act = jnp.square(jnp.maximum(up, 0.0)).astype(BF16)
    acc_ref[...] += jnp.dot(act, wd_ref[...], preferred_element_type=F32)

    @pl.when(f == pl.num_programs(1) - 1)
    def _():
        o_ref[...] = x_ref[...] + _rms(acc_ref[...], gpost_ref[...])


def _mlp(x, gpre, w_up, w_down, gpost, *, tm=512, tf=1024):
    T, D = x.shape
    F = w_up.shape[1]
    return pl.pallas_call(
        _mlp_kernel,
        grid=(T // tm, F // tf),
        in_specs=[pl.BlockSpec((tm, D), lambda i, f: (i, 0)),
                  pl.BlockSpec((1, D), lambda i, f: (0, 0)),
                  pl.BlockSpec((D, tf), lambda i, f: (0, f)),
                  pl.BlockSpec((tf, D), lambda i, f: (f, 0)),
                  pl.BlockSpec((1, D), lambda i, f: (0, 0))],
        out_specs=pl.BlockSpec((tm, D), lambda i, f: (i, 0)),
        out_shape=jax.ShapeDtypeStruct((T, D), F32),
        scratch_shapes=[pltpu.VMEM((tm, D), BF16), pltpu.VMEM((tm, D), F32)],
        compiler_params=_cparams(("parallel", "arbitrary")),
        name="mlp",
    )(x, gpre, w_up, w_down, gpost)


def _ple_kernel(x_ref, p_ref, gpre_ref, wg_ref, wp_ref, gpost_ref, o_ref):
    x = x_ref[...]
    h = _rms(x, gpre_ref[...]).astype(BF16)
    gate = _sigmoid(jnp.dot(h, wg_ref[...], preferred_element_type=F32))
    e = jnp.dot(p_ref[...].astype(BF16), wp_ref[...], preferred_element_type=F32) * gate
    o_ref[...] = x + _rms(e, gpost_ref[...])


def _ple(x, p, gpre, w_gate, w_proj, gpost, *, tm=512):
    T, D = x.shape
    return pl.pallas_call(
        _ple_kernel,
        grid=(T // tm,),
        in_specs=[pl.BlockSpec((tm, D), lambda i: (i, 0)),
                  pl.BlockSpec((tm, PLE_DIM), lambda i: (i, 0)),
                  pl.BlockSpec((1, D), lambda i: (0, 0)),
                  pl.BlockSpec((D, D), lambda i: (0, 0)),
                  pl.BlockSpec((PLE_DIM, D), lambda i: (0, 0)),
                  pl.BlockSpec((1, D), lambda i: (0, 0))],
        out_specs=pl.BlockSpec((tm, D), lambda i: (i, 0)),
        out_shape=jax.ShapeDtypeStruct((T, D), F32),
        compiler_params=_cparams(("parallel",)),
        name="ple",
    )(x, p, gpre, w_gate, w_proj, gpost)


def _relayout_w_in(w_in):
    D = w_in.shape[0]
    seg = lambda s: w_in[:, _IN_OFF[s]:_IN_OFF[s + 1]]
    cols = [seg(s) for s in _WIDE_ORDER] + [seg(s) for s in _SMALL_ORDER]
    used = _SMALL_COL + sum(IN_SPLITS[s] for s in _SMALL_ORDER)
    cols.append(jnp.zeros((D, IN_PAD - used), w_in.dtype))
    return jnp.concatenate(cols, axis=1).astype(BF16)


def _layer(x, p, pre_mix_norm, w_in, dn_conv_w, dn_a_log, dn_dt_bias, dn_norm,
           ssm_conv_w, ssm_conv_b, ssm_dt_bias, ssm_a_log, ssm_d, ssm_norm,
           gla_gate_w2, gla_gate_b, gla_norm, w_branch, w_out, post_mix_norm,
           pre_mlp_norm, w_up, w_down, post_mlp_norm,
           ple_pre_norm, w_ple_gate, w_ple_proj, ple_post_norm):
    row = lambda g: g.reshape(1, -1)
    proj = _inproj(x, row(pre_mix_norm), _relayout_w_in(w_in))
    y_dn = _dn_branch(proj, dn_conv_w, dn_a_log, dn_dt_bias, dn_norm)
    y_ssm = _ssd_branch(proj, ssm_conv_w, ssm_conv_b, ssm_dt_bias, ssm_a_log, ssm_d, ssm_norm)
    y_gla = _gla_branch(proj, gla_gate_w2, gla_gate_b, gla_norm)
    mixed_pre = _merge(y_dn, y_ssm, y_gla, proj, w_branch.astype(BF16))
    x = _outproj(mixed_pre, x, w_out.astype(BF16), row(post_mix_norm))
    x = _mlp(x, row(pre_mlp_norm), w_up.astype(BF16), w_down.astype(BF16), row(post_mlp_norm))
    x = _ple(x, p, row(ple_pre_norm), w_ple_gate.astype(BF16), w_ple_proj.astype(BF16),
             row(ple_post_norm))
    return x


def kernel(x, p, pre_mix_norm, w_in, dn_conv_w, dn_a_log, dn_dt_bias, dn_norm, ssm_conv_w, ssm_conv_b, ssm_dt_bias, ssm_a_log, ssm_d, ssm_norm, gla_gate_w2, gla_gate_b, gla_norm, w_branch, w_out, post_mix_norm, pre_mlp_norm, w_up, w_down, post_mlp_norm, ple_pre_norm, w_ple_gate, w_ple_proj, ple_post_norm):
    Bsz, T, D = x.shape
    depth = w_in.shape[0]
    outs = []
    for b in range(Bsz):
        xb = x[b]
        for i in range(depth):
            xb = _layer(xb, p[i, b], pre_mix_norm[i], w_in[i], dn_conv_w[i], dn_a_log[i],
                        dn_dt_bias[i], dn_norm[i], ssm_conv_w[i], ssm_conv_b[i], ssm_dt_bias[i],
                        ssm_a_log[i], ssm_d[i], ssm_norm[i], gla_gate_w2[i], gla_gate_b[i],
                        gla_norm[i], w_branch[i], w_out[i], post_mix_norm[i], pre_mlp_norm[i],
                        w_up[i], w_down[i], post_mlp_norm[i], ple_pre_norm[i], w_ple_gate[i],
                        w_ple_proj[i], ple_post_norm[i])
        outs.append(xb)
    return jnp.stack(outs, axis=0)
```

```python
import functools

import numpy as np
import jax
import jax.numpy as jnp
from jax import lax
from jax.experimental import pallas as pl
from jax.experimental.pallas import tpu as pltpu

F32 = jnp.float32
BF16 = jnp.bfloat16

D_MODEL = 2048
PLE_DIM = 256
NORM_EPS = 1e-6
CONV_WIDTH = 4
N_BRANCH = 3
MIX_WIDTH = D_MODEL // 2
D_FF = 4 * D_MODEL

DN_HEAD_DIM = 128
DN_HEADS = MIX_WIDTH // DN_HEAD_DIM
SSM_HEAD_DIM = 64
SSM_HEADS = MIX_WIDTH // SSM_HEAD_DIM
SSM_GROUPS = 2
SSM_STATE = 128
GLA_HEADS = 4
GLA_K_WIDTH = MIX_WIDTH // 2
GLA_K_DIM = GLA_K_WIDTH // GLA_HEADS
GLA_V_DIM = MIX_WIDTH // GLA_HEADS
GLA_GATE_RANK = 16
GLA_GATE_TEMP = 16.0

CHUNK = 64
LANES = 128
HALO = 8

IN_SPLITS = (
    MIX_WIDTH, MIX_WIDTH, MIX_WIDTH, DN_HEADS, DN_HEADS, MIX_WIDTH,
    MIX_WIDTH, MIX_WIDTH, SSM_GROUPS * SSM_STATE, SSM_GROUPS * SSM_STATE, SSM_HEADS,
    GLA_K_WIDTH, GLA_K_WIDTH, MIX_WIDTH, GLA_GATE_RANK, MIX_WIDTH,
    N_BRANCH * D_MODEL,
)
(_DN_Q, _DN_K, _DN_V, _DN_B, _DN_A, _DN_G, _S_Z, _S_X, _S_B, _S_C, _S_DT,
 _G_Q, _G_K, _G_V, _G_LR, _G_O, _BR) = range(17)
_IN_OFF = np.concatenate([[0], np.cumsum(IN_SPLITS)]).tolist()

_WIDE_ORDER = (_DN_Q, _DN_K, _DN_V, _DN_G, _S_Z, _S_X, _G_V, _G_O, _S_B, _S_C, _G_Q, _G_K, _BR)
_SMALL_ORDER = (_DN_B, _DN_A, _S_DT, _G_LR)
_COL = {}
_off = 0
for _s in _WIDE_ORDER:
    _COL[_s] = _off
    _off += IN_SPLITS[_s]
_SMALL_COL = _off
_SMALL_LANE = {}
_l = 0
for _s in _SMALL_ORDER:
    _SMALL_LANE[_s] = _l
    _l += IN_SPLITS[_s]
IN_PAD = 16384

VMEM_LIMIT = 50 * 1024 * 1024


def _cparams(sem):
    return pltpu.CompilerParams(dimension_semantics=sem, vmem_limit_bytes=VMEM_LIMIT)


def _mm(a, b):
    return jnp.dot(a.astype(BF16), b.astype(BF16), preferred_element_type=F32)


def _mm_nt(a, b):
    return lax.dot_general(a.astype(BF16), b.astype(BF16), (((1,), (1,)), ((), ())),
                           preferred_element_type=F32)


def _mm_tn(a, b):
    return lax.dot_general(a.astype(BF16), b.astype(BF16), (((0,), (0,)), ((), ())),
                           preferred_element_type=F32)


def _split3(x):
    hi = x.astype(BF16)
    r1 = x - hi.astype(F32)
    mid = r1.astype(BF16)
    lo = (r1 - mid.astype(F32)).astype(BF16)
    return hi, mid, lo


def _sel_left(a01, x):
    n = x.shape[1]
    r = jnp.dot(a01.astype(BF16), jnp.concatenate(_split3(x), axis=1), preferred_element_type=F32)
    return (r[:, :n] + r[:, n:2 * n]) + r[:, 2 * n:]


def _sel_right(x, b01):
    m = x.shape[0]
    r = jnp.dot(jnp.concatenate(_split3(x), axis=0), b01.astype(BF16), preferred_element_type=F32)
    return (r[:m] + r[m:2 * m]) + r[2 * m:]


def _sigmoid(x):
    return 1.0 / (1.0 + jnp.exp(-x))


def _silu(x):
    return x * _sigmoid(x)


def _softplus(x):
    return jnp.maximum(x, 0.0) + jnp.log(1.0 + jnp.exp(-jnp.abs(x)))


def _rms(x, gain):
    return x * lax.rsqrt(jnp.mean(x * x, axis=-1, keepdims=True) + NORM_EPS) * gain


def _iota(shape, dim):
    return lax.broadcasted_iota(jnp.int32, shape, dim)


def _tril_ones(n):
    return (_iota((n, n), 0) >= _iota((n, n), 1)).astype(F32)


def _conv_silu(x_ref, halo_ref, w, bias):
    tb = x_ref.shape[0]
    x = x_ref[...]
    halo_ref[HALO:HALO + tb, :] = x
    y = x * w[3:4, :]
    for k in range(CONV_WIDTH - 1):
        y = y + halo_ref[HALO - 3 + k:HALO - 3 + k + tb, :] * w[k:k + 1, :]
    halo_ref[0:HALO, :] = halo_ref[tb:tb + HALO, :]
    if bias is not None:
        y = y + bias
    return _silu(y)


def _inproj_kernel(x_ref, g_ref, w_ref, o_ref, h_ref):
    @pl.when(pl.program_id(1) == 0)
    def _():
        h_ref[...] = _rms(x_ref[...], g_ref[...]).astype(BF16)

    o_ref[...] = jnp.dot(h_ref[...], w_ref[...], preferred_element_type=F32)


def _inproj(x, gain, w, li, *, tm=1024, tn=1024):
    T, D = x.shape
    N = w.shape[2]
    return pl.pallas_call(
        _inproj_kernel,
        grid=(T // tm, N // tn),
        in_specs=[pl.BlockSpec((tm, D), lambda i, j: (i, 0)),
                  pl.BlockSpec((1, D), lambda i, j: (0, 0)),
                  pl.BlockSpec((None, D, tn), lambda i, j: (li, 0, j))],
        out_specs=pl.BlockSpec((tm, tn), lambda i, j: (i, j)),
        out_shape=jax.ShapeDtypeStruct((T, N), F32),
        scratch_shapes=[pltpu.VMEM((tm, D), BF16)],
        compiler_params=_cparams(("parallel", "arbitrary")),
        name="inproj",
    )(x, gain, w)


def _unit_lower_inverse_many(ms):
    n = ms[0].shape[0]
    idx = range(len(ms))
    r = _iota((n, n), 0)
    c = _iota((n, n), 1)
    same = (r // 16) == (c // 16)
    eye = (r == c).astype(F32)
    nd = [jnp.where(same, -m, 0.0) for m in ms]
    off = [jnp.where(same, 0.0, m) for m in ms]
    z = [eye + x for x in nd]
    p = [_mm(x, x) for x in nd]
    for _ in range(2):
        zp = [_mm(jnp.concatenate([z[i], p[i]], axis=0), p[i]) for i in idx]
        z = [z[i] + zp[i][:n] for i in idx]
        p = [zp[i][n:] for i in idx]
    xd = [z[i] + _mm(z[i], p[i]) for i in idx]
    n2 = [-_mm(xd[i], off[i]) for i in idx]
    z = [eye + x for x in n2]
    z = [z[i] + _mm(z[i], _mm(n2[i], n2[i])) for i in idx]
    return [_mm(z[i], xd[i]) for i in idx]


def _dn_kernel(q_ref, k_ref, v_ref, gate_ref, sm_ref, cw_ref, par_ref, nw_ref, o_ref,
               hq_ref, hk_ref, hv_ref, s_ref):
    C = CHUNK

    @pl.when(pl.program_id(0) == 0)
    def _():
        hq_ref[...] = jnp.zeros_like(hq_ref)
        hk_ref[...] = jnp.zeros_like(hk_ref)
        hv_ref[...] = jnp.zeros_like(hv_ref)
        s_ref[...] = jnp.zeros_like(s_ref)

    cw = cw_ref[...]
    qc = _conv_silu(q_ref, hq_ref, cw[:, 0:MIX_WIDTH], None)
    kc = _conv_silu(k_ref, hk_ref, cw[:, MIX_WIDTH:2 * MIX_WIDTH], None)
    vc = _conv_silu(v_ref, hv_ref, cw[:, 2 * MIX_WIDTH:3 * MIX_WIDTH], None)

    sm = sm_ref[...]
    lane = _iota((1, LANES), 1)
    a_lanes = (lane >= _SMALL_LANE[_DN_A]) & (lane < _SMALL_LANE[_DN_A] + DN_HEADS)
    a_neg = jnp.where(a_lanes, -jnp.exp(par_ref[0:1, :]), 0.0)
    beta_t = _sigmoid(sm)
    g_t = a_neg * _softplus(sm + par_ref[1:2, :])
    gc_t = _sel_left(_tril_ones(C), g_t)
    gc_rows = jnp.concatenate([gc_t, gc_t], axis=0).T

    r = _iota((C, C), 0)
    c = _iota((C, C), 1)
    causal = r >= c
    strict = r > c
    scale = DN_HEAD_DIM ** -0.5
    nw = nw_ref[...]

    H = range(DN_HEADS)
    sls = [slice(h * DN_HEAD_DIM, (h + 1) * DN_HEAD_DIM) for h in H]
    s_old = [s_ref[h] for h in H]
    gates = [gate_ref[:, sls[h]] for h in H]
    qh = [qc[:, sls[h]] for h in H]
    kh = [kc[:, sls[h]] for h in H]
    vh = [vc[:, sls[h]] for h in H]
    qh = [x * lax.rsqrt(jnp.sum(x * x, axis=-1, keepdims=True) + NORM_EPS) * scale for x in qh]
    kh = [x * lax.rsqrt(jnp.sum(x * x, axis=-1, keepdims=True) + NORM_EPS) for x in kh]
    beta = [beta_t[:, _SMALL_LANE[_DN_B] + h:_SMALL_LANE[_DN_B] + h + 1] for h in H]
    gcol = [gc_t[:, _SMALL_LANE[_DN_A] + h:_SMALL_LANE[_DN_A] + h + 1] for h in H]
    grow = [gc_rows[_SMALL_LANE[_DN_A] + h:_SMALL_LANE[_DN_A] + h + 1, :C] for h in H]
    glast = [g[C - 1:C, :] for g in gcol]
    decay = [jnp.where(causal, jnp.exp(gcol[h] - grow[h]), 0.0) for h in H]
    eg = [jnp.exp(g) for g in gcol]
    kb = [kh[h] * beta[h] for h in H]
    kq = [_mm_nt(jnp.concatenate([kb[h], qh[h]], axis=0), kh[h]) for h in H]
    m = [jnp.where(strict, kq[h][:C] * decay[h], 0.0) for h in H]
    attn = [kq[h][C:] * decay[h] for h in H]
    ainv = _unit_lower_inverse_many(m)
    uw = [_mm(ainv[h], jnp.concatenate([vh[h] * beta[h], kb[h] * eg[h]], axis=1)) for h in H]
    ws = [_mm(jnp.concatenate([uw[h][:, DN_HEAD_DIM:], qh[h] * eg[h]], axis=0), s_old[h]) for h in H]
    v_new = [uw[h][:, :DN_HEAD_DIM] - ws[h][:C] for h in H]
    o = [ws[h][C:] + _mm(attn[h], v_new[h]) for h in H]
    kd = [kh[h] * jnp.exp(glast[h] - gcol[h]) for h in H]
    s_new = [s_old[h] * jnp.exp(glast[h]) + _mm_tn(kd[h], v_new[h]) for h in H]
    outs = [(_rms(o[h], nw) * _silu(gates[h])).astype(o_ref.dtype) for h in H]
    for h in H:
        s_ref[h] = s_new[h]
    o_ref[...] = jnp.concatenate(outs, axis=1)


def _dn_branch(proj, conv_w, a_log, dt_bias, norm_w):
    T = proj.shape[0]
    C = CHUNK
    par = jnp.zeros((8, LANES), F32)
    la = _SMALL_LANE[_DN_A]
    par = par.at[0, la:la + DN_HEADS].set(a_log).at[1, la:la + DN_HEADS].set(dt_bias)
    wide = lambda seg: pl.BlockSpec((C, MIX_WIDTH), lambda i, b=_COL[seg] // MIX_WIDTH: (i, b))
    return pl.pallas_call(
        _dn_kernel,
        grid=(T // C,),
        in_specs=[wide(_DN_Q), wide(_DN_K), wide(_DN_V), wide(_DN_G),
                  pl.BlockSpec((C, LANES), lambda i: (i, _SMALL_COL // LANES)),
                  pl.BlockSpec((CONV_WIDTH, 3 * MIX_WIDTH), lambda i: (0, 0)),
                  pl.BlockSpec((8, LANES), lambda i: (0, 0)),
                  pl.BlockSpec((1, DN_HEAD_DIM), lambda i: (0, 0))],
        out_specs=pl.BlockSpec((C, MIX_WIDTH), lambda i: (i, 0)),
        out_shape=jax.ShapeDtypeStruct((T, MIX_WIDTH), BF16),
        scratch_shapes=[pltpu.VMEM((HALO + C, MIX_WIDTH), F32)] * 3
        + [pltpu.VMEM((DN_HEADS, DN_HEAD_DIM, DN_HEAD_DIM), F32)],
        compiler_params=_cparams(("arbitrary",)),
        name="deltanet",
    )(proj, proj, proj, proj, proj, conv_w, par, norm_w.reshape(1, DN_HEAD_DIM))


def _ssd_kernel(z_ref, x_ref, bc_ref, sm_ref, cw_ref, cb_ref, par_ref, dsk_ref, nw_ref, o_ref,
                hx_ref, hbc_ref, st_ref):
    C = CHUNK
    P = SSM_HEAD_DIM
    GW = MIX_WIDTH // SSM_GROUPS
    BCW = 2 * SSM_GROUPS * SSM_STATE

    @pl.when(pl.program_id(0) == 0)
    def _():
        hx_ref[...] = jnp.zeros_like(hx_ref)
        hbc_ref[...] = jnp.zeros_like(hbc_ref)
        st_ref[...] = jnp.zeros_like(st_ref)

    cw = cw_ref[...]
    cb = cb_ref[...]
    xc = _conv_silu(x_ref, hx_ref, cw[:, :MIX_WIDTH], cb[:, :MIX_WIDTH])
    bcc = _conv_silu(bc_ref, hbc_ref, cw[:, MIX_WIDTH:], cb[:, MIX_WIDTH:])

    sm = sm_ref[...]
    lane = _iota((1, LANES), 1)
    l0 = _SMALL_LANE[_S_DT]
    dt_lanes = (lane >= l0) & (lane < l0 + SSM_HEADS)
    a_neg = jnp.where(dt_lanes, -jnp.exp(par_ref[0:1, :]), 0.0)
    dt_t = jnp.where(dt_lanes, _softplus(sm + par_ref[1:2, :]), 0.0)
    acs_t = _sel_left(_tril_ones(C), dt_t * a_neg)
    er = _iota((LANES, MIX_WIDTH), 0)
    ec = _iota((LANES, MIX_WIDTH), 1)
    expand = ((er - l0) == (ec // P)).astype(F32)
    both = _sel_right(jnp.concatenate([acs_t, dt_t], axis=0), expand)
    acs = both[:C]
    dt_e = both[C:]
    rr = _iota((C, MIX_WIDTH), 0)
    cc = _iota((C, MIX_WIDTH), 1) % P
    row_acs = jnp.sum(jnp.where(rr == cc, acs, 0.0), axis=0, keepdims=True)
    last = acs[C - 1:C, :]
    lmat = jnp.where(rr >= cc, jnp.exp(acs - row_acs), 0.0)
    xdt = xc * dt_e
    e_acs = jnp.exp(acs)
    xw = xdt * jnp.exp(last - acs)
    e_last = jnp.exp(last)
    lane128 = _iota((C, LANES), 1)

    ys = []
    for g in range(SSM_GROUPS):
        bg = bcc[:, g * SSM_STATE:(g + 1) * SSM_STATE]
        cg = bcc[:, SSM_GROUPS * SSM_STATE + g * SSM_STATE:SSM_GROUPS * SSM_STATE + (g + 1) * SSM_STATE]
        cb2 = _mm_nt(cg, jnp.concatenate([bg, bg], axis=0))
        st = st_ref[g]
        gs = slice(g * GW, (g + 1) * GW)
        y_off = _mm(cg, st) * e_acs[:, gs]
        parts = []
        for pidx in range(GW // LANES):
            col = g * GW + pidx * LANES
            wmat = cb2 * lmat[:, col:col + LANES]
            xp = xdt[:, col:col + LANES]
            bd = jnp.concatenate([jnp.where(lane128 < P, xp, 0.0),
                                  jnp.where(lane128 >= P, xp, 0.0)], axis=0)
            parts.append(_mm(wmat, bd))
        ys.append(jnp.concatenate(parts, axis=1) + y_off)
        st_ref[g] = st * e_last[:, gs] + _mm_tn(bg, xw[:, gs])
    y = jnp.concatenate(ys, axis=1) + dsk_ref[...] * xc
    y = y * _silu(z_ref[...])
    nw = nw_ref[...]
    for g in range(SSM_GROUPS):
        gs = slice(g * GW, (g + 1) * GW)
        o_ref[:, gs] = _rms(y[:, gs], nw[:, gs]).astype(o_ref.dtype)


def _ssd_branch(proj, conv_w, conv_b, dt_bias, a_log, d_skip, norm_w):
    T = proj.shape[0]
    C = CHUNK
    BCW = 2 * SSM_GROUPS * SSM_STATE
    l0 = _SMALL_LANE[_S_DT]
    par = jnp.zeros((8, LANES), F32)
    par = par.at[0, l0:l0 + SSM_HEADS].set(a_log).at[1, l0:l0 + SSM_HEADS].set(dt_bias)
    dsk = jnp.repeat(d_skip, SSM_HEAD_DIM).reshape(1, MIX_WIDTH)
    wide = lambda seg: pl.BlockSpec((C, MIX_WIDTH), lambda i, b=_COL[seg] // MIX_WIDTH: (i, b))
    cw_total = MIX_WIDTH + BCW
    return pl.pallas_call(
        _ssd_kernel,
        grid=(T // C,),
        in_specs=[wide(_S_Z), wide(_S_X),
                  pl.BlockSpec((C, BCW), lambda i: (i, _COL[_S_B] // BCW)),
                  pl.BlockSpec((C, LANES), lambda i: (i, _SMALL_COL // LANES)),
                  pl.BlockSpec((CONV_WIDTH, cw_total), lambda i: (0, 0)),
                  pl.BlockSpec((1, cw_total), lambda i: (0, 0)),
                  pl.BlockSpec((8, LANES), lambda i: (0, 0)),
                  pl.BlockSpec((1, MIX_WIDTH), lambda i: (0, 0)),
                  pl.BlockSpec((1, MIX_WIDTH), lambda i: (0, 0))],
        out_specs=pl.BlockSpec((C, MIX_WIDTH), lambda i: (i, 0)),
        out_shape=jax.ShapeDtypeStruct((T, MIX_WIDTH), BF16),
        scratch_shapes=[pltpu.VMEM((HALO + C, MIX_WIDTH), F32),
                        pltpu.VMEM((HALO + C, BCW), F32),
                        pltpu.VMEM((SSM_GROUPS, SSM_STATE, MIX_WIDTH // SSM_GROUPS), F32)],
        compiler_params=_cparams(("arbitrary",)),
        name="ssd",
    )(proj, proj, proj, proj, conv_w, conv_b.reshape(1, cw_total), par, dsk,
      norm_w.reshape(1, MIX_WIDTH))


_GLA_LEVELS = (32, 16, 8, 4, 2, 1)


def _gla_cumsum_matrix():
    i = np.arange(CHUNK)[:, None]
    t = np.arange(CHUNK)[None, :]
    mats = [(t <= i), (t > i)]
    for b in _GLA_LEVELS[:-1]:
        same = (i // b) == (t // b)
        upper = ((i // b) % 2) == 1
        mats.append(np.where(upper, same & (t <= i), same & (t > i)))
    return np.concatenate(mats, axis=0).astype(np.float32)


def _gla_kernel(q_ref, k_ref, v_ref, og_ref, sm_ref, w2_ref, b2_ref, cm_ref, nw_ref, o_ref, st_ref):
    C = CHUNK

    @pl.when(pl.program_id(0) == 0)
    def _():
        st_ref[...] = jnp.zeros_like(st_ref)

    x = _mm(sm_ref[...], w2_ref[...]) + b2_ref[...]
    gk = -_softplus(-x) * (1.0 / GLA_GATE_TEMP)
    cs = _sel_left(cm_ref[...], gk)
    g_inc = cs[0:C]
    g_rev = cs[C:2 * C]
    r = _iota((C, C), 0)
    c = _iota((C, C), 1)
    rcol = _iota((C, 1), 0)
    scale = GLA_K_DIM ** -0.5
    nw = nw_ref[...]

    H = range(GLA_HEADS)
    ks = [slice(h * GLA_K_DIM, (h + 1) * GLA_K_DIM) for h in H]
    vs = [slice(h * GLA_V_DIM, (h + 1) * GLA_V_DIM) for h in H]
    q_all = q_ref[...] * scale
    k_all = k_ref[...]
    st = [st_ref[h] for h in H]
    v = [v_ref[:, vs[h]] for h in H]
    q = [q_all[:, ks[h]] for h in H]
    k = [k_all[:, ks[h]] for h in H]
    attn = [jnp.where(r == c, _mm_nt(q[h], k[h]), 0.0) for h in H]
    for lvl, b in enumerate(_GLA_LEVELS):
        upper = ((rcol // b) % 2) == 1
        if b > 1:
            base = C * (lvl + 2)
            fac = jnp.exp(cs[base:base + C])
        else:
            fac = jnp.exp(jnp.where(upper, gk, 0.0))
        qt_all = jnp.where(upper, q_all * fac, 0.0)
        kt_all = jnp.where(upper, 0.0, k_all * fac)
        pair = ((r // (2 * b)) == (c // (2 * b))) & (((r // b) % 2) == 1) & (((c // b) % 2) == 0)
        part = [_mm_nt(qt_all[:, ks[h]], kt_all[:, ks[h]]) for h in H]
        attn = [attn[h] + jnp.where(pair, part[h], 0.0) for h in H]
    qg_all = q_all * jnp.exp(g_inc)
    kd_all = k_all * jnp.exp(g_rev)
    dec = jnp.exp(g_inc[C - 1:C])
    o = [_mm_nt(qg_all[:, ks[h]], st[h]) + _mm(attn[h], v[h]) for h in H]
    st_new = [st[h] * dec[:, ks[h]] + _mm_tn(v[h], kd_all[:, ks[h]]) for h in H]
    outs = [(_rms(o[h], nw) * _silu(og_ref[:, vs[h]])).astype(o_ref.dtype) for h in H]
    for h in H:
        st_ref[h] = st_new[h]
    o_ref[...] = jnp.concatenate(outs, axis=1)


def _gla_branch(proj, w2, b2, norm_w):
    T = proj.shape[0]
    C = CHUNK
    l0 = _SMALL_LANE[_G_LR]
    w2p = jnp.zeros((LANES, GLA_K_WIDTH), F32).at[l0:l0 + GLA_GATE_RANK].set(w2)
    cm = jnp.asarray(_gla_cumsum_matrix(), BF16)
    wide = lambda seg: pl.BlockSpec((C, MIX_WIDTH), lambda i, b=_COL[seg] // MIX_WIDTH: (i, b))
    half = lambda seg: pl.BlockSpec((C, GLA_K_WIDTH), lambda i, b=_COL[seg] // GLA_K_WIDTH: (i, b))
    return pl.pallas_call(
        _gla_kernel,
        grid=(T // C,),
        in_specs=[half(_G_Q), half(_G_K), wide(_G_V), wide(_G_O),
                  pl.BlockSpec((C, LANES), lambda i: (i, _SMALL_COL // LANES)),
                  pl.BlockSpec((LANES, GLA_K_WIDTH), lambda i: (0, 0)),
                  pl.BlockSpec((1, GLA_K_WIDTH), lambda i: (0, 0)),
                  pl.BlockSpec(cm.shape, lambda i: (0, 0)),
                  pl.BlockSpec((1, GLA_V_DIM), lambda i: (0, 0))],
        out_specs=pl.BlockSpec((C, MIX_WIDTH), lambda i: (i, 0)),
        out_shape=jax.ShapeDtypeStruct((T, MIX_WIDTH), BF16),
        scratch_shapes=[pltpu.VMEM((GLA_HEADS, GLA_V_DIM, GLA_K_DIM), F32)],
        compiler_params=_cparams(("arbitrary",)),
        name="gla",
    )(proj, proj, proj, proj, proj, w2p, b2.reshape(1, GLA_K_WIDTH), cm,
      norm_w.reshape(1, GLA_V_DIM))


def _merge_kernel(ya_ref, yb_ref, yc_ref, ga_ref, gb_ref, gc_ref, w_ref, o_ref):
    acc = _sigmoid(ga_ref[...]) * jnp.dot(ya_ref[...], w_ref[0], preferred_element_type=F32)
    acc += _sigmoid(gb_ref[...]) * jnp.dot(yb_ref[...], w_ref[1], preferred_element_type=F32)
    acc += _sigmoid(gc_ref[...]) * jnp.dot(yc_ref[...], w_ref[2], preferred_element_type=F32)
    o_ref[...] = acc.astype(o_ref.dtype)


def _merge(y_dn, y_ssm, y_gla, proj, w_branch, li, *, tm=512, tn=512):
    T = proj.shape[0]
    ysp = pl.BlockSpec((tm, MIX_WIDTH), lambda i, j: (i, 0))
    gate = lambda b: pl.BlockSpec(
        (tm, tn), lambda i, j, base=(_COL[_BR] + b * D_MODEL) // tn: (i, base + j))
    return pl.pallas_call(
        _merge_kernel,
        grid=(T // tm, D_MODEL // tn),
        in_specs=[ysp, ysp, ysp, gate(0), gate(1), gate(2),
                  pl.BlockSpec((None, N_BRANCH, MIX_WIDTH, tn), lambda i, j: (li, 0, 0, j))],
        out_specs=pl.BlockSpec((tm, tn), lambda i, j: (i, j)),
        out_shape=jax.ShapeDtypeStruct((T, D_MODEL), BF16),
        compiler_params=_cparams(("parallel", "arbitrary")),
        name="merge",
    )(y_dn, y_ssm, y_gla, proj, proj, proj, w_branch)


def _outproj_kernel(m_ref, x_ref, w_ref, g_ref, o_ref):
    mixed = jnp.dot(m_ref[...], w_ref[...], preferred_element_type=F32)
    o_ref[...] = x_ref[...] + _rms(mixed, g_ref[...])


def _outproj(mixed_pre, x, w_out, li, gain, *, tm=512):
    T, D = x.shape
    return pl.pallas_call(
        _outproj_kernel,
        grid=(T // tm,),
        in_specs=[pl.BlockSpec((tm, D), lambda i: (i, 0)),
                  pl.BlockSpec((tm, D), lambda i: (i, 0)),
                  pl.BlockSpec((None, D, D), lambda i: (li, 0, 0)),
                  pl.BlockSpec((1, D), lambda i: (0, 0))],
        out_specs=pl.BlockSpec((tm, D), lambda i: (i, 0)),
        out_shape=jax.ShapeDtypeStruct((T, D), F32),
        compiler_params=_cparams(("parallel",)),
        name="outproj",
    )(mixed_pre, x, w_out, gain)


def _mlp_kernel(x_ref, gpre_ref, wu_ref, wd_ref, gpost_ref, o_ref, h_ref, acc_ref):
    f = pl.program_id(1)

    @pl.when(f == 0)
    def _():
        h_ref[...] = _rms(x_ref[...], gpre_ref[...]).astype(BF16)
        acc_ref[...] = jnp.zeros_like(acc_ref)

    up = jnp.dot(h_ref[...], wu_ref[...], preferred_element_type=F32)
    act = jnp.square(jnp.maximum(up, 0.0)).astype(BF16)
    acc_ref[...] += jnp.dot(act, wd_ref[...], preferred_element_type=F32)

    @pl.when(f == pl.num_programs(1) - 1)
    def _():
        o_ref[...] = x_ref[...] + _rms(acc_ref[...], gpost_ref[...])


def _mlp(x, gpre, w_up, w_down, li, gpost, *, tm=512, tf=1024):
    T, D = x.shape
    F = w_up.shape[2]
    return pl.pallas_call(
        _mlp_kernel,
        grid=(T // tm, F // tf),
        in_specs=[pl.BlockSpec((tm, D), lambda i, f: (i, 0)),
                  pl.BlockSpec((1, D), lambda i, f: (0, 0)),
                  pl.BlockSpec((None, D, tf), lambda i, f: (li, 0, f)),
                  pl.BlockSpec((None, tf, D), lambda i, f: (li, f, 0)),
                  pl.BlockSpec((1, D), lambda i, f: (0, 0))],
        out_specs=pl.BlockSpec((tm, D), lambda i, f: (i, 0)),
        out_shape=jax.ShapeDtypeStruct((T, D), F32),
        scratch_shapes=[pltpu.VMEM((tm, D), BF16), pltpu.VMEM((tm, D), F32)],
        compiler_params=_cparams(("parallel", "arbitrary")),
        name="mlp",
    )(x, gpre, w_up, w_down, gpost)


def _ple_kernel(x_ref, p_ref, gpre_ref, wg_ref, wp_ref, gpost_ref, o_ref):
    x = x_ref[...]
    h = _rms(x, gpre_ref[...]).astype(BF16)
    gate = _sigmoid(jnp.dot(h, wg_ref[...], preferred_element_type=F32))
    e = jnp.dot(p_ref[...].astype(BF16), wp_ref[...], preferred_element_type=F32) * gate
    o_ref[...] = x + _rms(e, gpost_ref[...])


def _ple(x, p, b, gpre, w_gate, w_proj, li, gpost, *, tm=512):
    T, D = x.shape
    return pl.pallas_call(
        _ple_kernel,
        grid=(T // tm,),
        in_specs=[pl.BlockSpec((tm, D), lambda i: (i, 0)),
                  pl.BlockSpec((None, None, tm, PLE_DIM), lambda i: (li, b, i, 0)),
                  pl.BlockSpec((1, D), lambda i: (0, 0)),
                  pl.BlockSpec((None, D, D), lambda i: (li, 0, 0)),
                  pl.BlockSpec((None, PLE_DIM, D), lambda i: (li, 0, 0)),
                  pl.BlockSpec((1, D), lambda i: (0, 0))],
        out_specs=pl.BlockSpec((tm, D), lambda i: (i, 0)),
        out_shape=jax.ShapeDtypeStruct((T, D), F32),
        compiler_params=_cparams(("parallel",)),
        name="ple",
    )(x, p, gpre, w_gate, w_proj, gpost)


def _relayout_w_in(w_in):
    depth, D, _ = w_in.shape
    w16 = w_in.astype(BF16)
    seg = lambda s: w16[:, :, _IN_OFF[s]:_IN_OFF[s + 1]]
    cols = [seg(s) for s in _WIDE_ORDER] + [seg(s) for s in _SMALL_ORDER]
    used = _SMALL_COL + sum(IN_SPLITS[s] for s in _SMALL_ORDER)
    cols.append(jnp.zeros((depth, D, IN_PAD - used), BF16))
    return jnp.concatenate(cols, axis=2)


def kernel(x, p, pre_mix_norm, w_in, dn_conv_w, dn_a_log, dn_dt_bias, dn_norm, ssm_conv_w, ssm_conv_b, ssm_dt_bias, ssm_a_log, ssm_d, ssm_norm, gla_gate_w2, gla_gate_b, gla_norm, w_branch, w_out, post_mix_norm, pre_mlp_norm, w_up, w_down, post_mlp_norm, ple_pre_norm, w_ple_gate, w_ple_proj, ple_post_norm):
    Bsz, T, D = x.shape
    depth = w_in.shape[0]
    row = lambda g: g.reshape(1, -1)
    w_in16 = _relayout_w_in(w_in)
    w_branch16 = w_branch.astype(BF16)
    w_out16 = w_out.astype(BF16)
    w_up16 = w_up.astype(BF16)
    w_down16 = w_down.astype(BF16)
    w_pg16 = w_ple_gate.astype(BF16)
    w_pp16 = w_ple_proj.astype(BF16)
    outs = []
    for b in range(Bsz):
        xb = x[b]
        for i in range(depth):
            proj = _inproj(xb, row(pre_mix_norm[i]), w_in16, i)
            y_dn = _dn_branch(proj, dn_conv_w[i], dn_a_log[i], dn_dt_bias[i], dn_norm[i])
            y_ssm = _ssd_branch(proj, ssm_conv_w[i], ssm_conv_b[i], ssm_dt_bias[i], ssm_a_log[i],
                                ssm_d[i], ssm_norm[i])
            y_gla = _gla_branch(proj, gla_gate_w2[i], gla_gate_b[i], gla_norm[i])
            mixed_pre = _merge(y_dn, y_ssm, y_gla, proj, w_branch16, i)
            xb = _outproj(mixed_pre, xb, w_out16, i, row(post_mix_norm[i]))
            xb = _mlp(xb, row(pre_mlp_norm[i]), w_up16, w_down16, i, row(post_mlp_norm[i]))
            xb = _ple(xb, p, b, row(ple_pre_norm[i]), w_pg16, w_pp16, i, row(ple_post_norm[i]))
        outs.append(xb)
    return jnp.stack(outs, axis=0)
```

```python
import functools

import numpy as np
import jax
import jax.numpy as jnp
from jax import lax
from jax.experimental import pallas as pl
from jax.experimental.pallas import tpu as pltpu

F32 = jnp.float32
BF16 = jnp.bfloat16

D_MODEL = 2048
PLE_DIM = 256
NORM_EPS = 1e-6
CONV_WIDTH = 4
N_BRANCH = 3
MIX_WIDTH = D_MODEL // 2
D_FF = 4 * D_MODEL

DN_HEAD_DIM = 128
DN_HEADS = MIX_WIDTH // DN_HEAD_DIM
SSM_HEAD_DIM = 64
SSM_HEADS = MIX_WIDTH // SSM_HEAD_DIM
SSM_GROUPS = 2
SSM_STATE = 128
GLA_HEADS = 4
GLA_K_WIDTH = MIX_WIDTH // 2
GLA_K_DIM = GLA_K_WIDTH // GLA_HEADS
GLA_V_DIM = MIX_WIDTH // GLA_HEADS
GLA_GATE_RANK = 16
GLA_GATE_TEMP = 16.0

CHUNK = 64
MIXER_CHUNKS_PER_STEP = 2
LANES = 128
HALO = 8

IN_SPLITS = (
    MIX_WIDTH, MIX_WIDTH, MIX_WIDTH, DN_HEADS, DN_HEADS, MIX_WIDTH,
    MIX_WIDTH, MIX_WIDTH, SSM_GROUPS * SSM_STATE, SSM_GROUPS * SSM_STATE, SSM_HEADS,
    GLA_K_WIDTH, GLA_K_WIDTH, MIX_WIDTH, GLA_GATE_RANK, MIX_WIDTH,
    N_BRANCH * D_MODEL,
)
(_DN_Q, _DN_K, _DN_V, _DN_B, _DN_A, _DN_G, _S_Z, _S_X, _S_B, _S_C, _S_DT,
 _G_Q, _G_K, _G_V, _G_LR, _G_O, _BR) = range(17)
_IN_OFF = np.concatenate([[0], np.cumsum(IN_SPLITS)]).tolist()

_WIDE_ORDER = (_DN_Q, _DN_K, _DN_V, _DN_G, _S_Z, _S_X, _G_V, _G_O, _S_B, _S_C, _G_Q, _G_K, _BR)
_SMALL_ORDER = (_DN_B, _DN_A, _S_DT, _G_LR)
_COL = {}
_off = 0
for _s in _WIDE_ORDER:
    _COL[_s] = _off
    _off += IN_SPLITS[_s]
_SMALL_COL = _off
_SMALL_LANE = {}
_l = 0
for _s in _SMALL_ORDER:
    _SMALL_LANE[_s] = _l
    _l += IN_SPLITS[_s]
IN_PAD = 16384

VMEM_LIMIT = 50 * 1024 * 1024


def _cparams(sem):
    return pltpu.CompilerParams(dimension_semantics=sem, vmem_limit_bytes=VMEM_LIMIT)


def _mm(a, b):
    return jnp.dot(a.astype(BF16), b.astype(BF16), preferred_element_type=F32)


def _mm_nt(a, b):
    return lax.dot_general(a.astype(BF16), b.astype(BF16), (((1,), (1,)), ((), ())),
                           preferred_element_type=F32)


def _mm_tn(a, b):
    return lax.dot_general(a.astype(BF16), b.astype(BF16), (((0,), (0,)), ((), ())),
                           preferred_element_type=F32)


def _split3(x):
    hi = x.astype(BF16)
    r1 = x - hi.astype(F32)
    mid = r1.astype(BF16)
    lo = (r1 - mid.astype(F32)).astype(BF16)
    return hi, mid, lo


def _sel_left(a01, x):
    n = x.shape[1]
    r = jnp.dot(a01.astype(BF16), jnp.concatenate(_split3(x), axis=1), preferred_element_type=F32)
    return (r[:, :n] + r[:, n:2 * n]) + r[:, 2 * n:]


def _sel_right(x, b01):
    m = x.shape[0]
    r = jnp.dot(jnp.concatenate(_split3(x), axis=0), b01.astype(BF16), preferred_element_type=F32)
    return (r[:m] + r[m:2 * m]) + r[2 * m:]


def _sigmoid(x):
    return 0.5 * jnp.tanh(0.5 * x) + 0.5


def _silu(x):
    return x * _sigmoid(x)


def _softplus(x):
    return jnp.maximum(x, 0.0) + jnp.log(1.0 + jnp.exp(-jnp.abs(x)))


def _rms(x, gain):
    return x * lax.rsqrt(jnp.mean(x * x, axis=-1, keepdims=True) + NORM_EPS) * gain


def _iota(shape, dim):
    return lax.broadcasted_iota(jnp.int32, shape, dim)


def _tril_ones(n):
    return (_iota((n, n), 0) >= _iota((n, n), 1)).astype(F32)


def _conv_silu(x_ref, halo_ref, w, bias):
    tb = x_ref.shape[0]
    x = x_ref[...]
    halo_ref[HALO:HALO + tb, :] = x
    y = x * w[3:4, :]
    for k in range(CONV_WIDTH - 1):
        y = y + halo_ref[HALO - 3 + k:HALO - 3 + k + tb, :] * w[k:k + 1, :]
    halo_ref[0:HALO, :] = halo_ref[tb:tb + HALO, :]
    if bias is not None:
        y = y + bias
    return _silu(y)


def _inproj_kernel(x_ref, g_ref, w_ref, o_ref, h_ref):
    @pl.when(pl.program_id(1) == 0)
    def _():
        h_ref[...] = _rms(x_ref[...], g_ref[...]).astype(BF16)

    o_ref[...] = jnp.dot(h_ref[...], w_ref[...], preferred_element_type=F32)


def _inproj(x, gain, w, li, *, tm=1024, tn=1024):
    T, D = x.shape
    N = w.shape[2]
    return pl.pallas_call(
        _inproj_kernel,
        grid=(T // tm, N // tn),
        in_specs=[pl.BlockSpec((tm, D), lambda i, j: (i, 0)),
                  pl.BlockSpec((1, D), lambda i, j: (0, 0)),
                  pl.BlockSpec((None, D, tn), lambda i, j: (li, 0, j))],
        out_specs=pl.BlockSpec((tm, tn), lambda i, j: (i, j)),
        out_shape=jax.ShapeDtypeStruct((T, N), F32),
        scratch_shapes=[pltpu.VMEM((tm, D), BF16)],
        compiler_params=_cparams(("parallel", "arbitrary")),
        name="inproj",
    )(x, gain, w)


def _unit_lower_inverse_many(ms):
    n = ms[0].shape[0]
    idx = range(len(ms))
    r = _iota((n, n), 0)
    c = _iota((n, n), 1)
    same = (r // 16) == (c // 16)
    eye = (r == c).astype(F32)
    nd = [jnp.where(same, -m, 0.0) for m in ms]
    off = [jnp.where(same, 0.0, m) for m in ms]
    z = [eye + x for x in nd]
    p = [_mm(x, x) for x in nd]
    for _ in range(2):
        zp = [_mm(jnp.concatenate([z[i], p[i]], axis=0), p[i]) for i in idx]
        z = [z[i] + zp[i][:n] for i in idx]
        p = [zp[i][n:] for i in idx]
    xd = [z[i] + _mm(z[i], p[i]) for i in idx]
    n2 = [-_mm(xd[i], off[i]) for i in idx]
    z = [eye + x for x in n2]
    z = [z[i] + _mm(z[i], _mm(n2[i], n2[i])) for i in idx]
    return [_mm(z[i], xd[i]) for i in idx]


def _dn_kernel(q_ref, k_ref, v_ref, gate_ref, sm_ref, cw_ref, par_ref, nw_ref, o_ref,
               hq_ref, hk_ref, hv_ref, s_ref):
    C = CHUNK
    nc = q_ref.shape[0] // C

    @pl.when(pl.program_id(0) == 0)
    def _():
        hq_ref[...] = jnp.zeros_like(hq_ref)
        hk_ref[...] = jnp.zeros_like(hk_ref)
        hv_ref[...] = jnp.zeros_like(hv_ref)
        s_ref[...] = jnp.zeros_like(s_ref)

    cw = cw_ref[...]
    qc = _conv_silu(q_ref, hq_ref, cw[:, 0:MIX_WIDTH], None)
    kc = _conv_silu(k_ref, hk_ref, cw[:, MIX_WIDTH:2 * MIX_WIDTH], None)
    vc = _conv_silu(v_ref, hv_ref, cw[:, 2 * MIX_WIDTH:3 * MIX_WIDTH], None)

    sm = sm_ref[...]
    lane = _iota((1, LANES), 1)
    a_lanes = (lane >= _SMALL_LANE[_DN_A]) & (lane < _SMALL_LANE[_DN_A] + DN_HEADS)
    a_neg = jnp.where(a_lanes, -jnp.exp(par_ref[0:1, :]), 0.0)
    beta_t = _sigmoid(sm)
    g_t = a_neg * _softplus(sm + par_ref[1:2, :])

    r = _iota((C, C), 0)
    c = _iota((C, C), 1)
    causal = r >= c
    strict = r > c
    scale = DN_HEAD_DIM ** -0.5
    nw = nw_ref[...]
    tril = _tril_ones(C)

    H = range(DN_HEADS)
    CH = [(ci, h) for ci in range(nc) for h in H]
    rows = [slice(ci * C, (ci + 1) * C) for ci in range(nc)]
    sls = [slice(h * DN_HEAD_DIM, (h + 1) * DN_HEAD_DIM) for h in H]
    bl = [_SMALL_LANE[_DN_B] + h for h in H]
    gl = [_SMALL_LANE[_DN_A] + h for h in H]

    gc_t = [_sel_left(tril, g_t[rows[ci]]) for ci in range(nc)]
    gc_rows = [jnp.concatenate([g, g], axis=0).T for g in gc_t]
    qn = [qc[:, sls[h]] for h in H]
    kn = [kc[:, sls[h]] for h in H]
    qn = [x * lax.rsqrt(jnp.sum(x * x, axis=-1, keepdims=True) + NORM_EPS) * scale for x in qn]
    kn = [x * lax.rsqrt(jnp.sum(x * x, axis=-1, keepdims=True) + NORM_EPS) for x in kn]
    qh = {(ci, h): qn[h][rows[ci]] for ci, h in CH}
    kh = {(ci, h): kn[h][rows[ci]] for ci, h in CH}
    vh = {(ci, h): vc[rows[ci], sls[h]] for ci, h in CH}
    beta = {(ci, h): beta_t[rows[ci], bl[h]:bl[h] + 1] for ci, h in CH}
    gcol = {(ci, h): gc_t[ci][:, gl[h]:gl[h] + 1] for ci, h in CH}
    grow = {(ci, h): gc_rows[ci][gl[h]:gl[h] + 1, :C] for ci, h in CH}
    glast = {k_: gcol[k_][C - 1:C, :] for k_ in CH}
    decay = {k_: jnp.where(causal, jnp.exp(gcol[k_] - grow[k_]), 0.0) for k_ in CH}
    eg = {k_: jnp.exp(gcol[k_]) for k_ in CH}
    kb = {k_: kh[k_] * beta[k_] for k_ in CH}
    kq = {k_: _mm_nt(jnp.concatenate([kb[k_], qh[k_]], axis=0), kh[k_]) for k_ in CH}
    m = [jnp.where(strict, kq[k_][:C] * decay[k_], 0.0) for k_ in CH]
    attn = {k_: kq[k_][C:] * decay[k_] for k_ in CH}
    ainv = dict(zip(CH, _unit_lower_inverse_many(m)))
    uw = {k_: _mm(ainv[k_], jnp.concatenate([vh[k_] * beta[k_], kb[k_] * eg[k_]], axis=1)) for k_ in CH}
    wq = {k_: jnp.concatenate([uw[k_][:, DN_HEAD_DIM:], qh[k_] * eg[k_]], axis=0) for k_ in CH}
    kd = {k_: kh[k_] * jnp.exp(glast[k_] - gcol[k_]) for k_ in CH}
    egl = {k_: jnp.exp(glast[k_]) for k_ in CH}

    s_cur = [s_ref[h] for h in H]
    out_rows = []
    for ci in range(nc):
        ws = [_mm(wq[ci, h], s_cur[h]) for h in H]
        v_new = [uw[ci, h][:, :DN_HEAD_DIM] - ws[h][:C] for h in H]
        o = [ws[h][C:] + _mm(attn[ci, h], v_new[h]) for h in H]
        s_cur = [s_cur[h] * egl[ci, h] + _mm_tn(kd[ci, h], v_new[h]) for h in H]
        out_rows.append(jnp.concatenate(
            [(_rms(o[h], nw) * _silu(gate_ref[rows[ci], sls[h]])).astype(o_ref.dtype) for h in H],
            axis=1))
    for h in H:
        s_ref[h] = s_cur[h]
    o_ref[...] = jnp.concatenate(out_rows, axis=0)


def _dn_branch(proj, conv_w, a_log, dt_bias, norm_w, *, nc=MIXER_CHUNKS_PER_STEP):
    T = proj.shape[0]
    C = nc * CHUNK
    par = jnp.zeros((8, LANES), F32)
    la = _SMALL_LANE[_DN_A]
    par = par.at[0, la:la + DN_HEADS].set(a_log).at[1, la:la + DN_HEADS].set(dt_bias)
    wide = lambda seg: pl.BlockSpec((C, MIX_WIDTH), lambda i, b=_COL[seg] // MIX_WIDTH: (i, b))
    return pl.pallas_call(
        _dn_kernel,
        grid=(T // C,),
        in_specs=[wide(_DN_Q), wide(_DN_K), wide(_DN_V), wide(_DN_G),
                  pl.BlockSpec((C, LANES), lambda i: (i, _SMALL_COL // LANES)),
                  pl.BlockSpec((CONV_WIDTH, 3 * MIX_WIDTH), lambda i: (0, 0)),
                  pl.BlockSpec((8, LANES), lambda i: (0, 0)),
                  pl.BlockSpec((1, DN_HEAD_DIM), lambda i: (0, 0))],
        out_specs=pl.BlockSpec((C, MIX_WIDTH), lambda i: (i, 0)),
        out_shape=jax.ShapeDtypeStruct((T, MIX_WIDTH), BF16),
        scratch_shapes=[pltpu.VMEM((HALO + C, MIX_WIDTH), F32)] * 3
        + [pltpu.VMEM((DN_HEADS, DN_HEAD_DIM, DN_HEAD_DIM), F32)],
        compiler_params=_cparams(("arbitrary",)),
        name="deltanet",
    )(proj, proj, proj, proj, proj, conv_w, par, norm_w.reshape(1, DN_HEAD_DIM))


def _ssd_kernel(z_ref, x_ref, bc_ref, sm_ref, cw_ref, cb_ref, par_ref, dsk_ref, nw_ref, o_ref,
                hx_ref, hbc_ref, st_ref):
    C = CHUNK
    TB = x_ref.shape[0]
    nc = TB // C
    P = SSM_HEAD_DIM
    GW = MIX_WIDTH // SSM_GROUPS
    NS = SSM_GROUPS * SSM_STATE

    @pl.when(pl.program_id(0) == 0)
    def _():
        hx_ref[...] = jnp.zeros_like(hx_ref)
        hbc_ref[...] = jnp.zeros_like(hbc_ref)
        st_ref[...] = jnp.zeros_like(st_ref)

    cw = cw_ref[...]
    cb = cb_ref[...]
    xc = _conv_silu(x_ref, hx_ref, cw[:, :MIX_WIDTH], cb[:, :MIX_WIDTH])
    bcc = _conv_silu(bc_ref, hbc_ref, cw[:, MIX_WIDTH:], cb[:, MIX_WIDTH:])

    sm = sm_ref[...]
    lane = _iota((1, LANES), 1)
    l0 = _SMALL_LANE[_S_DT]
    dt_lanes = (lane >= l0) & (lane < l0 + SSM_HEADS)
    a_neg = jnp.where(dt_lanes, -jnp.exp(par_ref[0:1, :]), 0.0)
    dt_t = jnp.where(dt_lanes, _softplus(sm + par_ref[1:2, :]), 0.0)
    tr = _iota((TB, TB), 0)
    tc = _iota((TB, TB), 1)
    blocktril = ((tr >= tc) & ((tr // C) == (tc // C))).astype(F32)
    acs_t = _sel_left(blocktril, dt_t * a_neg)
    er = _iota((LANES, MIX_WIDTH), 0)
    ec = _iota((LANES, MIX_WIDTH), 1)
    expand = ((er - l0) == (ec // P)).astype(F32)
    both = _sel_right(jnp.concatenate([acs_t, dt_t], axis=0), expand)
    acs_all = both[:TB]
    dt_e = both[TB:]
    rr = _iota((C, MIX_WIDTH), 0)
    cc = _iota((C, MIX_WIDTH), 1) % P
    lane128 = _iota((C, LANES), 1)
    xdt_all = xc * dt_e
    e_acs_all = jnp.exp(acs_all)

    rows = [slice(ci * C, (ci + 1) * C) for ci in range(nc)]
    CI = range(nc)
    G = range(SSM_GROUPS)
    gs = [slice(g * GW, (g + 1) * GW) for g in G]
    acs = [acs_all[rows[ci]] for ci in CI]
    xdt = [xdt_all[rows[ci]] for ci in CI]
    row_acs = [jnp.sum(jnp.where(rr == cc, acs[ci], 0.0), axis=0, keepdims=True) for ci in CI]
    last = [acs[ci][C - 1:C, :] for ci in CI]
    lmat = [jnp.where(rr >= cc, jnp.exp(acs[ci] - row_acs[ci]), 0.0) for ci in CI]
    xw = [xdt[ci] * jnp.exp(last[ci] - acs[ci]) for ci in CI]
    e_last = [jnp.exp(last[ci]) for ci in CI]
    bg = {(ci, g): bcc[rows[ci], g * SSM_STATE:(g + 1) * SSM_STATE] for ci in CI for g in G}
    cg = {(ci, g): bcc[rows[ci], NS + g * SSM_STATE:NS + (g + 1) * SSM_STATE] for ci in CI for g in G}
    cb2 = {k_: _mm_nt(cg[k_], jnp.concatenate([bg[k_], bg[k_]], axis=0)) for k_ in bg}
    upd = {(ci, g): _mm_tn(bg[ci, g], xw[ci][:, gs[g]]) for ci in CI for g in G}
    ydiag = {}
    for ci in CI:
        for g in G:
            parts = []
            for pidx in range(GW // LANES):
                col = g * GW + pidx * LANES
                wmat = cb2[ci, g] * lmat[ci][:, col:col + LANES]
                xp = xdt[ci][:, col:col + LANES]
                bd = jnp.concatenate([jnp.where(lane128 < P, xp, 0.0),
                                      jnp.where(lane128 >= P, xp, 0.0)], axis=0)
                parts.append(_mm(wmat, bd))
            ydiag[ci, g] = jnp.concatenate(parts, axis=1)

    st_cur = [st_ref[g] for g in G]
    y_rows = []
    for ci in CI:
        y_off = [_mm(cg[ci, g], st_cur[g]) * e_acs_all[rows[ci], gs[g]] for g in G]
        st_cur = [st_cur[g] * e_last[ci][:, gs[g]] + upd[ci, g] for g in G]
        y_rows.append(jnp.concatenate([ydiag[ci, g] + y_off[g] for g in G], axis=1))
    for g in G:
        st_ref[g] = st_cur[g]
    y = jnp.concatenate(y_rows, axis=0) + dsk_ref[...] * xc
    y = y * _silu(z_ref[...])
    nw = nw_ref[...]
    for g in G:
        o_ref[:, gs[g]] = _rms(y[:, gs[g]], nw[:, gs[g]]).astype(o_ref.dtype)


def _ssd_branch(proj, conv_w, conv_b, dt_bias, a_log, d_skip, norm_w, *, nc=MIXER_CHUNKS_PER_STEP):
    T = proj.shape[0]
    C = nc * CHUNK
    BCW = 2 * SSM_GROUPS * SSM_STATE
    l0 = _SMALL_LANE[_S_DT]
    par = jnp.zeros((8, LANES), F32)
    par = par.at[0, l0:l0 + SSM_HEADS].set(a_log).at[1, l0:l0 + SSM_HEADS].set(dt_bias)
    dsk = jnp.repeat(d_skip, SSM_HEAD_DIM).reshape(1, MIX_WIDTH)
    wide = lambda seg: pl.BlockSpec((C, MIX_WIDTH), lambda i, b=_COL[seg] // MIX_WIDTH: (i, b))
    cw_total = MIX_WIDTH + BCW
    return pl.pallas_call(
        _ssd_kernel,
        grid=(T // C,),
        in_specs=[wide(_S_Z), wide(_S_X),
                  pl.BlockSpec((C, BCW), lambda i: (i, _COL[_S_B] // BCW)),
                  pl.BlockSpec((C, LANES), lambda i: (i, _SMALL_COL // LANES)),
                  pl.BlockSpec((CONV_WIDTH, cw_total), lambda i: (0, 0)),
                  pl.BlockSpec((1, cw_total), lambda i: (0, 0)),
                  pl.BlockSpec((8, LANES), lambda i: (0, 0)),
                  pl.BlockSpec((1, MIX_WIDTH), lambda i: (0, 0)),
                  pl.BlockSpec((1, MIX_WIDTH), lambda i: (0, 0))],
        out_specs=pl.BlockSpec((C, MIX_WIDTH), lambda i: (i, 0)),
        out_shape=jax.ShapeDtypeStruct((T, MIX_WIDTH), BF16),
        scratch_shapes=[pltpu.VMEM((HALO + C, MIX_WIDTH), F32),
                        pltpu.VMEM((HALO + C, BCW), F32),
                        pltpu.VMEM((SSM_GROUPS, SSM_STATE, MIX_WIDTH // SSM_GROUPS), F32)],
        compiler_params=_cparams(("arbitrary",)),
        name="ssd",
    )(proj, proj, proj, proj, conv_w, conv_b.reshape(1, cw_total), par, dsk,
      norm_w.reshape(1, MIX_WIDTH))


_GLA_LEVELS = (32, 16, 8, 4, 2, 1)


def _gla_cumsum_matrix():
    i = np.arange(CHUNK)[:, None]
    t = np.arange(CHUNK)[None, :]
    mats = [(t <= i), (t > i)]
    for b in _GLA_LEVELS[:-1]:
        same = (i // b) == (t // b)
        upper = ((i // b) % 2) == 1
        mats.append(np.where(upper, same & (t <= i), same & (t > i)))
    return np.concatenate(mats, axis=0).astype(np.float32)


def _gla_kernel(q_ref, k_ref, v_ref, og_ref, sm_ref, w2_ref, b2_ref, cm_ref, nw_ref, o_ref, st_ref):
    C = CHUNK

    @pl.when(pl.program_id(0) == 0)
    def _():
        st_ref[...] = jnp.zeros_like(st_ref)

    nc = q_ref.shape[0] // C
    x = _mm(sm_ref[...], w2_ref[...]) + b2_ref[...]
    gk_all = -_softplus(-x) * (1.0 / GLA_GATE_TEMP)
    r = _iota((C, C), 0)
    c = _iota((C, C), 1)
    rcol = _iota((C, 1), 0)
    scale = GLA_K_DIM ** -0.5
    nw = nw_ref[...]
    cm = cm_ref[...]
    q_blk = q_ref[...] * scale
    k_blk = k_ref[...]

    H = range(GLA_HEADS)
    CI = range(nc)
    CH = [(ci, h) for ci in CI for h in H]
    rows = [slice(ci * C, (ci + 1) * C) for ci in CI]
    ks = [slice(h * GLA_K_DIM, (h + 1) * GLA_K_DIM) for h in H]
    vs = [slice(h * GLA_V_DIM, (h + 1) * GLA_V_DIM) for h in H]
    gk = [gk_all[rows[ci]] for ci in CI]
    cs = [_sel_left(cm, gk[ci]) for ci in CI]
    q_all = [q_blk[rows[ci]] for ci in CI]
    k_all = [k_blk[rows[ci]] for ci in CI]
    v = {(ci, h): v_ref[rows[ci], vs[h]] for ci, h in CH}
    attn = {(ci, h): jnp.where(r == c, _mm_nt(q_all[ci][:, ks[h]], k_all[ci][:, ks[h]]), 0.0)
            for ci, h in CH}
    for lvl, b in enumerate(_GLA_LEVELS):
        upper = ((rcol // b) % 2) == 1
        pair = ((r // (2 * b)) == (c // (2 * b))) & (((r // b) % 2) == 1) & (((c // b) % 2) == 0)
        if b > 1:
            base = C * (lvl + 2)
            fac = [jnp.exp(cs[ci][base:base + C]) for ci in CI]
        else:
            fac = [jnp.exp(jnp.where(upper, gk[ci], 0.0)) for ci in CI]
        qt = [jnp.where(upper, q_all[ci] * fac[ci], 0.0) for ci in CI]
        kt = [jnp.where(upper, 0.0, k_all[ci] * fac[ci]) for ci in CI]
        part = {(ci, h): _mm_nt(qt[ci][:, ks[h]], kt[ci][:, ks[h]]) for ci, h in CH}
        attn = {k_: attn[k_] + jnp.where(pair, part[k_], 0.0) for k_ in CH}
    qg = [q_all[ci] * jnp.exp(cs[ci][0:C]) for ci in CI]
    kd = [k_all[ci] * jnp.exp(cs[ci][C:2 * C]) for ci in CI]
    dec = [jnp.exp(cs[ci][C - 1:C]) for ci in CI]
    intra = {k_: _mm(attn[k_], v[k_]) for k_ in CH}
    upd = {(ci, h): _mm_tn(v[ci, h], kd[ci][:, ks[h]]) for ci, h in CH}

    st_cur = [st_ref[h] for h in H]
    out_rows = []
    for ci in CI:
        o = [_mm_nt(qg[ci][:, ks[h]], st_cur[h]) + intra[ci, h] for h in H]
        st_cur = [st_cur[h] * dec[ci][:, ks[h]] + upd[ci, h] for h in H]
        out_rows.append(jnp.concatenate(
            [(_rms(o[h], nw) * _silu(og_ref[rows[ci], vs[h]])).astype(o_ref.dtype) for h in H], axis=1))
    for h in H:
        st_ref[h] = st_cur[h]
    o_ref[...] = jnp.concatenate(out_rows, axis=0)


def _gla_branch(proj, w2, b2, norm_w, *, nc=MIXER_CHUNKS_PER_STEP):
    T = proj.shape[0]
    C = nc * CHUNK
    l0 = _SMALL_LANE[_G_LR]
    w2p = jnp.zeros((LANES, GLA_K_WIDTH), F32).at[l0:l0 + GLA_GATE_RANK].set(w2)
    cm = jnp.asarray(_gla_cumsum_matrix(), BF16)
    wide = lambda seg: pl.BlockSpec((C, MIX_WIDTH), lambda i, b=_COL[seg] // MIX_WIDTH: (i, b))
    half = lambda seg: pl.BlockSpec((C, GLA_K_WIDTH), lambda i, b=_COL[seg] // GLA_K_WIDTH: (i, b))
    return pl.pallas_call(
        _gla_kernel,
        grid=(T // C,),
        in_specs=[half(_G_Q), half(_G_K), wide(_G_V), wide(_G_O),
                  pl.BlockSpec((C, LANES), lambda i: (i, _SMALL_COL // LANES)),
                  pl.BlockSpec((LANES, GLA_K_WIDTH), lambda i: (0, 0)),
                  pl.BlockSpec((1, GLA_K_WIDTH), lambda i: (0, 0)),
                  pl.BlockSpec(cm.shape, lambda i: (0, 0)),
                  pl.BlockSpec((1, GLA_V_DIM), lambda i: (0, 0))],
        out_specs=pl.BlockSpec((C, MIX_WIDTH), lambda i: (i, 0)),
        out_shape=jax.ShapeDtypeStruct((T, MIX_WIDTH), BF16),
        scratch_shapes=[pltpu.VMEM((GLA_HEADS, GLA_V_DIM, GLA_K_DIM), F32)],
        compiler_params=_cparams(("arbitrary",)),
        name="gla",
    )(proj, proj, proj, proj, proj, w2p, b2.reshape(1, GLA_K_WIDTH), cm,
      norm_w.reshape(1, GLA_V_DIM))


def _merge_kernel(ya_ref, yb_ref, yc_ref, ga_ref, gb_ref, gc_ref, w_ref, o_ref):
    acc = _sigmoid(ga_ref[...]) * jnp.dot(ya_ref[...], w_ref[0], preferred_element_type=F32)
    acc += _sigmoid(gb_ref[...]) * jnp.dot(yb_ref[...], w_ref[1], preferred_element_type=F32)
    acc += _sigmoid(gc_ref[...]) * jnp.dot(yc_ref[...], w_ref[2], preferred_element_type=F32)
    o_ref[...] = acc.astype(o_ref.dtype)


def _merge(y_dn, y_ssm, y_gla, proj, w_branch, li, *, tm=1024, tn=512):
    T = proj.shape[0]
    ysp = pl.BlockSpec((tm, MIX_WIDTH), lambda i, j: (i, 0))
    gate = lambda b: pl.BlockSpec(
        (tm, tn), lambda i, j, base=(_COL[_BR] + b * D_MODEL) // tn: (i, base + j))
    return pl.pallas_call(
        _merge_kernel,
        grid=(T // tm, D_MODEL // tn),
        in_specs=[ysp, ysp, ysp, gate(0), gate(1), gate(2),
                  pl.BlockSpec((None, N_BRANCH, MIX_WIDTH, tn), lambda i, j: (li, 0, 0, j))],
        out_specs=pl.BlockSpec((tm, tn), lambda i, j: (i, j)),
        out_shape=jax.ShapeDtypeStruct((T, D_MODEL), BF16),
        compiler_params=_cparams(("parallel", "arbitrary")),
        name="merge",
    )(y_dn, y_ssm, y_gla, proj, proj, proj, w_branch)


def _outproj_kernel(m_ref, x_ref, w_ref, g_ref, o_ref):
    mixed = jnp.dot(m_ref[...], w_ref[...], preferred_element_type=F32)
    o_ref[...] = x_ref[...] + _rms(mixed, g_ref[...])


def _outproj(mixed_pre, x, w_out, li, gain, *, tm=512):
    T, D = x.shape
    return pl.pallas_call(
        _outproj_kernel,
        grid=(T // tm,),
        in_specs=[pl.BlockSpec((tm, D), lambda i: (i, 0)),
                  pl.BlockSpec((tm, D), lambda i: (i, 0)),
                  pl.BlockSpec((None, D, D), lambda i: (li, 0, 0)),
                  pl.BlockSpec((1, D), lambda i: (0, 0))],
        out_specs=pl.BlockSpec((tm, D), lambda i: (i, 0)),
        out_shape=jax.ShapeDtypeStruct((T, D), F32),
        compiler_params=_cparams(("parallel",)),
        name="outproj",
    )(mixed_pre, x, w_out, gain)


def _mlp_kernel(x_ref, gpre_ref, wu_ref, wd_ref, gpost_ref, o_ref, h_ref, acc_ref):
    f = pl.program_id(1)

    @pl.when(f == 0)
    def _():
        h_ref[...] = _rms(x_ref[...], gpre_ref[...]).astype(BF16)
        acc_ref[...] = jnp.zeros_like(acc_ref)

    up = jnp.dot(h_ref[...], wu_ref[...], preferred_element_type=F32)
    act = jnp.square(jnp.maximum(up, 0.0)).astype(BF16)
    acc_ref[...] += jnp.dot(act, wd_ref[...], preferred_element_type=F32)

    @pl.when(f == pl.num_programs(1) - 1)
    def _():
        o_ref[...] = x_ref[...] + _rms(acc_ref[...], gpost_ref[...])


def _mlp(x, gpre, w_up, w_down, li, gpost, *, tm=512, tf=1024):
    T, D = x.shape
    F = w_up.shape[2]
    return pl.pallas_call(
        _mlp_kernel,
        grid=(T // tm, F // tf),
        in_specs=[pl.BlockSpec((tm, D), lambda i, f: (i, 0)),
                  pl.BlockSpec((1, D), lambda i, f: (0, 0)),
                  pl.BlockSpec((None, D, tf), lambda i, f: (li, 0, f)),
                  pl.BlockSpec((None, tf, D), lambda i, f: (li, f, 0)),
                  pl.BlockSpec((1, D), lambda i, f: (0, 0))],
        out_specs=pl.BlockSpec((tm, D), lambda i, f: (i, 0)),
        out_shape=jax.ShapeDtypeStruct((T, D), F32),
        scratch_shapes=[pltpu.VMEM((tm, D), BF16), pltpu.VMEM((tm, D), F32)],
        compiler_params=_cparams(("parallel", "arbitrary")),
        name="mlp",
    )(x, gpre, w_up, w_down, gpost)


def _ple_kernel(x_ref, p_ref, gpre_ref, wg_ref, wp_ref, gpost_ref, o_ref):
    x = x_ref[...]
    h = _rms(x, gpre_ref[...]).astype(BF16)
    gate = _sigmoid(jnp.dot(h, wg_ref[...], preferred_element_type=F32))
    e = jnp.dot(p_ref[...].astype(BF16), wp_ref[...], preferred_element_type=F32) * gate
    o_ref[...] = x + _rms(e, gpost_ref[...])


def _ple(x, p, b, gpre, w_gate, w_proj, li, gpost, *, tm=512):
    T, D = x.shape
    return pl.pallas_call(
        _ple_kernel,
        grid=(T // tm,),
        in_specs=[pl.BlockSpec((tm, D), lambda i: (i, 0)),
                  pl.BlockSpec((None, None, tm, PLE_DIM), lambda i: (li, b, i, 0)),
                  pl.BlockSpec((1, D), lambda i: (0, 0)),
                  pl.BlockSpec((None, D, D), lambda i: (li, 0, 0)),
                  pl.BlockSpec((None, PLE_DIM, D), lambda i: (li, 0, 0)),
                  pl.BlockSpec((1, D), lambda i: (0, 0))],
        out_specs=pl.BlockSpec((tm, D), lambda i: (i, 0)),
        out_shape=jax.ShapeDtypeStruct((T, D), F32),
        compiler_params=_cparams(("parallel",)),
        name="ple",
    )(x, p, gpre, w_gate, w_proj, gpost)


RELAY_TN = 512
RELAY_SHIFTS = (0, 16, 32, 48)


def _relayout_tables():
    src = np.full((_SMALL_COL,), -1, np.int64)
    for s in _WIDE_ORDER:
        src[_COL[s]:_COL[s] + IN_SPLITS[s]] = np.arange(_IN_OFF[s], _IN_OFF[s + 1])
    starts = src[::RELAY_TN]
    for j, s0 in enumerate(starts):
        assert np.array_equal(src[j * RELAY_TN:(j + 1) * RELAY_TN], s0 + np.arange(RELAY_TN))
        assert s0 % RELAY_TN in RELAY_SHIFTS
    blk = np.concatenate([starts // RELAY_TN, [0]]).astype(np.int32)
    sh = np.concatenate([starts % RELAY_TN, [0]]).astype(np.int32)
    return blk, sh


def _relayout_kernel(blk_ref, sh_ref, a_ref, b_ref, small_ref, o_ref):
    j = pl.program_id(1)
    last = pl.num_programs(1) - 1

    @pl.when(j == last)
    def _():
        o_ref[...] = small_ref[...].astype(BF16)

    for sh in RELAY_SHIFTS:
        @pl.when((j < last) & (sh_ref[j] == sh))
        def _(sh=sh):
            if sh == 0:
                o_ref[...] = a_ref[...].astype(BF16)
            else:
                cat = jnp.concatenate([a_ref[...], b_ref[...]], axis=1)
                o_ref[...] = cat[:, sh:sh + RELAY_TN].astype(BF16)


def _relayout_w_in(w_in):
    depth, D, _ = w_in.shape
    blk, sh = _relayout_tables()
    nj = blk.shape[0]
    assert nj * RELAY_TN == IN_PAD
    small = jnp.concatenate([w_in[:, :, _IN_OFF[s]:_IN_OFF[s + 1]] for s in _SMALL_ORDER], axis=2)
    small = jnp.pad(small, ((0, 0), (0, 0), (0, RELAY_TN - small.shape[2])))
    sub = RELAY_TN // LANES
    grid_spec = pltpu.PrefetchScalarGridSpec(
        num_scalar_prefetch=2,
        grid=(depth, nj),
        in_specs=[pl.BlockSpec((None, D, RELAY_TN), lambda l, j, blk, sh: (l, 0, blk[j])),
                  pl.BlockSpec((None, D, LANES), lambda l, j, blk, sh: (l, 0, (blk[j] + 1) * sub)),
                  pl.BlockSpec((None, D, RELAY_TN), lambda l, j, blk, sh: (l, 0, 0))],
        out_specs=pl.BlockSpec((None, D, RELAY_TN), lambda l, j, blk, sh: (l, 0, j)),
    )
    return pl.pallas_call(
        _relayout_kernel,
        grid_spec=grid_spec,
        out_shape=jax.ShapeDtypeStruct((depth, D, IN_PAD), BF16),
        compiler_params=_cparams(("arbitrary", "arbitrary")),
        name="relayout_w_in",
    )(jnp.asarray(blk), jnp.asarray(sh), w_in, w_in, small)


def kernel(x, p, pre_mix_norm, w_in, dn_conv_w, dn_a_log, dn_dt_bias, dn_norm, ssm_conv_w, ssm_conv_b, ssm_dt_bias, ssm_a_log, ssm_d, ssm_norm, gla_gate_w2, gla_gate_b, gla_norm, w_branch, w_out, post_mix_norm, pre_mlp_norm, w_up, w_down, post_mlp_norm, ple_pre_norm, w_ple_gate, w_ple_proj, ple_post_norm):
    Bsz, T, D = x.shape
    depth = w_in.shape[0]
    row = lambda g: g.reshape(1, -1)
    w_in16 = _relayout_w_in(w_in)
    w_branch16 = w_branch.astype(BF16)
    w_out16 = w_out.astype(BF16)
    w_up16 = w_up.astype(BF16)
    w_down16 = w_down.astype(BF16)
    w_pg16 = w_ple_gate.astype(BF16)
    w_pp16 = w_ple_proj.astype(BF16)
    outs = []
    for b in range(Bsz):
        xb = x[b]
        for i in range(depth):
            proj = _inproj(xb, row(pre_mix_norm[i]), w_in16, i)
            y_dn = _dn_branch(proj, dn_conv_w[i], dn_a_log[i], dn_dt_bias[i], dn_norm[i])
            y_ssm = _ssd_branch(proj, ssm_conv_w[i], ssm_conv_b[i], ssm_dt_bias[i], ssm_a_log[i],
                                ssm_d[i], ssm_norm[i])
            y_gla = _gla_branch(proj, gla_gate_w2[i], gla_gate_b[i], gla_norm[i])
            mixed_pre = _merge(y_dn, y_ssm, y_gla, proj, w_branch16, i)
            xb = _outproj(mixed_pre, xb, w_out16, i, row(post_mix_norm[i]))
            xb = _mlp(xb, row(pre_mlp_norm[i]), w_up16, w_down16, i, row(post_mlp_norm[i]))
            xb = _ple(xb, p, b, row(ple_pre_norm[i]), w_pg16, w_pp16, i, row(ple_post_norm[i]))
        outs.append(xb)
    return jnp.stack(outs, axis=0)
```

```python
import functools

import numpy as np
import jax
import jax.numpy as jnp
from jax import lax
from jax.experimental import pallas as pl
from jax.experimental.pallas import tpu as pltpu

F32 = jnp.float32
BF16 = jnp.bfloat16

D_MODEL = 2048
PLE_DIM = 256
NORM_EPS = 1e-6
CONV_WIDTH = 4
N_BRANCH = 3
MIX_WIDTH = D_MODEL // 2
D_FF = 4 * D_MODEL

DN_HEAD_DIM = 128
DN_HEADS = MIX_WIDTH // DN_HEAD_DIM
SSM_HEAD_DIM = 64
SSM_HEADS = MIX_WIDTH // SSM_HEAD_DIM
SSM_GROUPS = 2
SSM_STATE = 128
GLA_HEADS = 4
GLA_K_WIDTH = MIX_WIDTH // 2
GLA_K_DIM = GLA_K_WIDTH // GLA_HEADS
GLA_V_DIM = MIX_WIDTH // GLA_HEADS
GLA_GATE_RANK = 16
GLA_GATE_TEMP = 16.0

CHUNK = 64
MIXER_CHUNKS_PER_STEP = 2
LANES = 128
HALO = 8

IN_SPLITS = (
    MIX_WIDTH, MIX_WIDTH, MIX_WIDTH, DN_HEADS, DN_HEADS, MIX_WIDTH,
    MIX_WIDTH, MIX_WIDTH, SSM_GROUPS * SSM_STATE, SSM_GROUPS * SSM_STATE, SSM_HEADS,
    GLA_K_WIDTH, GLA_K_WIDTH, MIX_WIDTH, GLA_GATE_RANK, MIX_WIDTH,
    N_BRANCH * D_MODEL,
)
(_DN_Q, _DN_K, _DN_V, _DN_B, _DN_A, _DN_G, _S_Z, _S_X, _S_B, _S_C, _S_DT,
 _G_Q, _G_K, _G_V, _G_LR, _G_O, _BR) = range(17)
_IN_OFF = np.concatenate([[0], np.cumsum(IN_SPLITS)]).tolist()

_WIDE_ORDER = (_DN_Q, _DN_K, _DN_V, _DN_G, _S_Z, _S_X, _G_V, _G_O, _S_B, _S_C, _G_Q, _G_K, _BR)
_SMALL_ORDER = (_DN_B, _DN_A, _S_DT, _G_LR)
_COL = {}
_off = 0
for _s in _WIDE_ORDER:
    _COL[_s] = _off
    _off += IN_SPLITS[_s]
_SMALL_COL = _off
_SMALL_LANE = {}
_l = 0
for _s in _SMALL_ORDER:
    _SMALL_LANE[_s] = _l
    _l += IN_SPLITS[_s]
IN_PAD = 16384

VMEM_LIMIT = 50 * 1024 * 1024


def _cparams(sem):
    return pltpu.CompilerParams(dimension_semantics=sem, vmem_limit_bytes=VMEM_LIMIT)


def _mm(a, b):
    return jnp.dot(a.astype(BF16), b.astype(BF16), preferred_element_type=F32)


def _mm_nt(a, b):
    return lax.dot_general(a.astype(BF16), b.astype(BF16), (((1,), (1,)), ((), ())),
                           preferred_element_type=F32)


def _mm_tn(a, b):
    return lax.dot_general(a.astype(BF16), b.astype(BF16), (((0,), (0,)), ((), ())),
                           preferred_element_type=F32)


def _split3(x):
    hi = x.astype(BF16)
    r1 = x - hi.astype(F32)
    mid = r1.astype(BF16)
    lo = (r1 - mid.astype(F32)).astype(BF16)
    return hi, mid, lo


def _sel_left(a01, x):
    n = x.shape[1]
    r = jnp.dot(a01.astype(BF16), jnp.concatenate(_split3(x), axis=1), preferred_element_type=F32)
    return (r[:, :n] + r[:, n:2 * n]) + r[:, 2 * n:]


def _sel_right(x, b01):
    m = x.shape[0]
    r = jnp.dot(jnp.concatenate(_split3(x), axis=0), b01.astype(BF16), preferred_element_type=F32)
    return (r[:m] + r[m:2 * m]) + r[2 * m:]


def _sigmoid(x):
    return 0.5 * jnp.tanh(0.5 * x) + 0.5


def _silu(x):
    return x * _sigmoid(x)


def _softplus(x):
    return jnp.maximum(x, 0.0) + jnp.log(1.0 + jnp.exp(-jnp.abs(x)))


def _rms(x, gain):
    return x * lax.rsqrt(jnp.mean(x * x, axis=-1, keepdims=True) + NORM_EPS) * gain


def _iota(shape, dim):
    return lax.broadcasted_iota(jnp.int32, shape, dim)


def _tril_ones(n):
    return (_iota((n, n), 0) >= _iota((n, n), 1)).astype(F32)


CONV_SLAB = 256


def _conv_silu(x_ref, halo_ref, w, bias):
    tb, width = x_ref.shape
    slabs = []
    for c0 in range(0, width, CONV_SLAB):
        cs = slice(c0, c0 + CONV_SLAB)
        x = x_ref[:, cs]
        halo_ref[HALO:HALO + tb, cs] = x
        y = x * w[3:4, cs]
        for k in range(CONV_WIDTH - 1):
            y = y + halo_ref[HALO - 3 + k:HALO - 3 + k + tb, cs] * w[k:k + 1, cs]
        halo_ref[0:HALO, cs] = halo_ref[tb:tb + HALO, cs]
        if bias is not None:
            y = y + bias[:, cs]
        slabs.append(_silu(y))
        yield
    return jnp.concatenate(slabs, axis=1)


def _inproj_kernel(x_ref, g_ref, w_ref, o_ref, h_ref):
    @pl.when(pl.program_id(1) == 0)
    def _():
        h_ref[...] = _rms(x_ref[...], g_ref[...]).astype(BF16)

    o_ref[...] = lax.dot_general(h_ref[...], w_ref[...], (((1,), (1,)), ((), ())),
                                 preferred_element_type=F32)


def _inproj(x, gain, wt, li, *, tm=1024, tn=1024):
    T, D = x.shape
    N = wt.shape[1]
    return pl.pallas_call(
        _inproj_kernel,
        grid=(T // tm, N // tn),
        in_specs=[pl.BlockSpec((tm, D), lambda i, j: (i, 0)),
                  pl.BlockSpec((1, D), lambda i, j: (0, 0)),
                  pl.BlockSpec((None, tn, D), lambda i, j: (li, j, 0))],
        out_specs=pl.BlockSpec((tm, tn), lambda i, j: (i, j)),
        out_shape=jax.ShapeDtypeStruct((T, N), F32),
        scratch_shapes=[pltpu.VMEM((tm, D), BF16)],
        compiler_params=_cparams(("parallel", "arbitrary")),
        name="inproj",
    )(x, gain, wt)


def _unit_lower_inverse_many(ms):
    n = ms[0].shape[0]
    idx = range(len(ms))
    r = _iota((n, n), 0)
    c = _iota((n, n), 1)
    same = (r // 16) == (c // 16)
    eye = (r == c).astype(F32)
    nd = [jnp.where(same, -m, 0.0) for m in ms]
    off = [jnp.where(same, 0.0, m) for m in ms]
    z = [eye + x for x in nd]
    p = [_mm(x, x) for x in nd]
    yield
    for _ in range(2):
        zp = [_mm(jnp.concatenate([z[i], p[i]], axis=0), p[i]) for i in idx]
        z = [z[i] + zp[i][:n] for i in idx]
        p = [zp[i][n:] for i in idx]
        yield
    xd = [z[i] + _mm(z[i], p[i]) for i in idx]
    yield
    n2 = [-_mm(xd[i], off[i]) for i in idx]
    yield
    z = [eye + x for x in n2]
    n4 = [_mm(n2[i], n2[i]) for i in idx]
    yield
    z = [z[i] + _mm(z[i], n4[i]) for i in idx]
    yield
    return [_mm(z[i], xd[i]) for i in idx]


def _dn_stages(q_ref, k_ref, v_ref, gate_ref, sm_ref, cw_ref, par_ref, nw_ref, o_ref,
               hq_ref, hk_ref, hv_ref, s_ref):
    C = CHUNK
    nc = q_ref.shape[0] // C

    @pl.when(pl.program_id(0) == 0)
    def _():
        hq_ref[...] = jnp.zeros_like(hq_ref)
        hk_ref[...] = jnp.zeros_like(hk_ref)
        hv_ref[...] = jnp.zeros_like(hv_ref)
        s_ref[...] = jnp.zeros_like(s_ref)

    cw = cw_ref[...]
    qc = yield from _conv_silu(q_ref, hq_ref, cw[:, 0:MIX_WIDTH], None)
    kc = yield from _conv_silu(k_ref, hk_ref, cw[:, MIX_WIDTH:2 * MIX_WIDTH], None)
    vc = yield from _conv_silu(v_ref, hv_ref, cw[:, 2 * MIX_WIDTH:3 * MIX_WIDTH], None)

    sm = sm_ref[...]
    lane = _iota((1, LANES), 1)
    a_lanes = (lane >= _SMALL_LANE[_DN_A]) & (lane < _SMALL_LANE[_DN_A] + DN_HEADS)
    a_neg = jnp.where(a_lanes, -jnp.exp(par_ref[0:1, :]), 0.0)
    beta_t = _sigmoid(sm)
    g_t = a_neg * _softplus(sm + par_ref[1:2, :])

    r = _iota((C, C), 0)
    c = _iota((C, C), 1)
    causal = r >= c
    strict = r > c
    scale = DN_HEAD_DIM ** -0.5
    nw = nw_ref[...]
    tril = _tril_ones(C)

    H = range(DN_HEADS)
    CH = [(ci, h) for ci in range(nc) for h in H]
    rows = [slice(ci * C, (ci + 1) * C) for ci in range(nc)]
    sls = [slice(h * DN_HEAD_DIM, (h + 1) * DN_HEAD_DIM) for h in H]
    bl = [_SMALL_LANE[_DN_B] + h for h in H]
    gl = [_SMALL_LANE[_DN_A] + h for h in H]

    gc_t = [_sel_left(tril, g_t[rows[ci]]) for ci in range(nc)]
    gc_rows = [jnp.concatenate([g, g], axis=0).T for g in gc_t]
    yield
    qn = [qc[:, sls[h]] for h in H]
    kn = [kc[:, sls[h]] for h in H]
    qn = [x * lax.rsqrt(jnp.sum(x * x, axis=-1, keepdims=True) + NORM_EPS) * scale for x in qn]
    yield
    kn = [x * lax.rsqrt(jnp.sum(x * x, axis=-1, keepdims=True) + NORM_EPS) for x in kn]
    yield
    qh = {(ci, h): qn[h][rows[ci]] for ci, h in CH}
    kh = {(ci, h): kn[h][rows[ci]] for ci, h in CH}
    vh = {(ci, h): vc[rows[ci], sls[h]] for ci, h in CH}
    beta = {(ci, h): beta_t[rows[ci], bl[h]:bl[h] + 1] for ci, h in CH}
    gcol = {(ci, h): gc_t[ci][:, gl[h]:gl[h] + 1] for ci, h in CH}
    grow = {(ci, h): gc_rows[ci][gl[h]:gl[h] + 1, :C] for ci, h in CH}
    glast = {k_: gcol[k_][C - 1:C, :] for k_ in CH}
    decay = {k_: jnp.where(causal, jnp.exp(gcol[k_] - grow[k_]), 0.0) for k_ in CH}
    eg = {k_: jnp.exp(gcol[k_]) for k_ in CH}
    kb = {k_: kh[k_] * beta[k_] for k_ in CH}
    yield
    kq = {k_: _mm_nt(jnp.concatenate([kb[k_], qh[k_]], axis=0), kh[k_]) for k_ in CH}
    yield
    m = [jnp.where(strict, kq[k_][:C] * decay[k_], 0.0) for k_ in CH]
    attn = {k_: kq[k_][C:] * decay[k_] for k_ in CH}
    yield
    ainv = dict(zip(CH, (yield from _unit_lower_inverse_many(m))))
    uw = {k_: _mm(ainv[k_], jnp.concatenate([vh[k_] * beta[k_], kb[k_] * eg[k_]], axis=1)) for k_ in CH}
    yield
    wq = {k_: jnp.concatenate([uw[k_][:, DN_HEAD_DIM:], qh[k_] * eg[k_]], axis=0) for k_ in CH}
    kd = {k_: kh[k_] * jnp.exp(glast[k_] - gcol[k_]) for k_ in CH}
    egl = {k_: jnp.exp(glast[k_]) for k_ in CH}
    yield

    s_cur = [s_ref[h] for h in H]
    out_rows = []
    for ci in range(nc):
        ws = [_mm(wq[ci, h], s_cur[h]) for h in H]
        yield
        v_new = [uw[ci, h][:, :DN_HEAD_DIM] - ws[h][:C] for h in H]
        o = [ws[h][C:] + _mm(attn[ci, h], v_new[h]) for h in H]
        yield
        s_cur = [s_cur[h] * egl[ci, h] + _mm_tn(kd[ci, h], v_new[h]) for h in H]
        yield
        out_rows.append(jnp.concatenate(
            [(_rms(o[h], nw) * _silu(gate_ref[rows[ci], sls[h]])).astype(o_ref.dtype) for h in H],
            axis=1))
        yield
    for h in H:
        s_ref[h] = s_cur[h]
    o_ref[...] = jnp.concatenate(out_rows, axis=0)


def _ssd_stages(z_ref, x_ref, bc_ref, sm_ref, cw_ref, cb_ref, par_ref, dsk_ref, nw_ref, o_ref,
                hx_ref, hbc_ref, st_ref):
    C = CHUNK
    TB = x_ref.shape[0]
    nc = TB // C
    P = SSM_HEAD_DIM
    GW = MIX_WIDTH // SSM_GROUPS
    NS = SSM_GROUPS * SSM_STATE

    @pl.when(pl.program_id(0) == 0)
    def _():
        hx_ref[...] = jnp.zeros_like(hx_ref)
        hbc_ref[...] = jnp.zeros_like(hbc_ref)
        st_ref[...] = jnp.zeros_like(st_ref)

    cw = cw_ref[...]
    cb = cb_ref[...]
    xc = yield from _conv_silu(x_ref, hx_ref, cw[:, :MIX_WIDTH], cb[:, :MIX_WIDTH])
    bcc = yield from _conv_silu(bc_ref, hbc_ref, cw[:, MIX_WIDTH:], cb[:, MIX_WIDTH:])

    sm = sm_ref[...]
    lane = _iota((1, LANES), 1)
    l0 = _SMALL_LANE[_S_DT]
    dt_lanes = (lane >= l0) & (lane < l0 + SSM_HEADS)
    a_neg = jnp.where(dt_lanes, -jnp.exp(par_ref[0:1, :]), 0.0)
    dt_t = jnp.where(dt_lanes, _softplus(sm + par_ref[1:2, :]), 0.0)
    tr = _iota((TB, TB), 0)
    tc = _iota((TB, TB), 1)
    blocktril = ((tr >= tc) & ((tr // C) == (tc // C))).astype(F32)
    acs_t = _sel_left(blocktril, dt_t * a_neg)
    yield
    er = _iota((LANES, MIX_WIDTH), 0)
    ec = _iota((LANES, MIX_WIDTH), 1)
    expand = ((er - l0) == (ec // P)).astype(F32)
    both = _sel_right(jnp.concatenate([acs_t, dt_t], axis=0), expand)
    acs_all = both[:TB]
    dt_e = both[TB:]
    rr = _iota((C, MIX_WIDTH), 0)
    cc = _iota((C, MIX_WIDTH), 1) % P
    lane128 = _iota((C, LANES), 1)
    xdt_all = xc * dt_e
    e_acs_all = jnp.exp(acs_all)
    yield

    rows = [slice(ci * C, (ci + 1) * C) for ci in range(nc)]
    CI = range(nc)
    G = range(SSM_GROUPS)
    gs = [slice(g * GW, (g + 1) * GW) for g in G]
    acs = [acs_all[rows[ci]] for ci in CI]
    xdt = [xdt_all[rows[ci]] for ci in CI]
    row_acs = [jnp.sum(jnp.where(rr == cc, acs[ci], 0.0), axis=0, keepdims=True) for ci in CI]
    last = [acs[ci][C - 1:C, :] for ci in CI]
    lmat = [jnp.where(rr >= cc, jnp.exp(acs[ci] - row_acs[ci]), 0.0) for ci in CI]
    xw = [xdt[ci] * jnp.exp(last[ci] - acs[ci]) for ci in CI]
    e_last = [jnp.exp(last[ci]) for ci in CI]
    yield
    bg = {(ci, g): bcc[rows[ci], g * SSM_STATE:(g + 1) * SSM_STATE] for ci in CI for g in G}
    cg = {(ci, g): bcc[rows[ci], NS + g * SSM_STATE:NS + (g + 1) * SSM_STATE] for ci in CI for g in G}
    cb2 = {k_: _mm_nt(cg[k_], jnp.concatenate([bg[k_], bg[k_]], axis=0)) for k_ in bg}
    upd = {(ci, g): _mm_tn(bg[ci, g], xw[ci][:, gs[g]]) for ci in CI for g in G}
    yield
    ydiag = {}
    for ci in CI:
        for g in G:
            parts = []
            for pidx in range(GW // LANES):
                col = g * GW + pidx * LANES
                wmat = cb2[ci, g] * lmat[ci][:, col:col + LANES]
                xp = xdt[ci][:, col:col + LANES]
                bd = jnp.concatenate([jnp.where(lane128 < P, xp, 0.0),
                                      jnp.where(lane128 >= P, xp, 0.0)], axis=0)
                parts.append(_mm(wmat, bd))
            ydiag[ci, g] = jnp.concatenate(parts, axis=1)
            yield

    st_cur = [st_ref[g] for g in G]
    y_rows = []
    for ci in CI:
        y_off = [_mm(cg[ci, g], st_cur[g]) * e_acs_all[rows[ci], gs[g]] for g in G]
        st_cur = [st_cur[g] * e_last[ci][:, gs[g]] + upd[ci, g] for g in G]
        y_rows.append(jnp.concatenate([ydiag[ci, g] + y_off[g] for g in G], axis=1))
        yield
    for g in G:
        st_ref[g] = st_cur[g]
    y = jnp.concatenate(y_rows, axis=0) + dsk_ref[...] * xc
    y = y * _silu(z_ref[...])
    yield
    nw = nw_ref[...]
    for g in G:
        o_ref[:, gs[g]] = _rms(y[:, gs[g]], nw[:, gs[g]]).astype(o_ref.dtype)


_GLA_LEVELS = (32, 16, 8, 4, 2, 1)


def _gla_cumsum_matrix():
    i = np.arange(CHUNK)[:, None]
    t = np.arange(CHUNK)[None, :]
    mats = [(t <= i), (t > i)]
    for b in _GLA_LEVELS[:-1]:
        same = (i // b) == (t // b)
        upper = ((i // b) % 2) == 1
        mats.append(np.where(upper, same & (t <= i), same & (t > i)))
    return np.concatenate(mats, axis=0).astype(np.float32)


def _gla_stages(q_ref, k_ref, v_ref, og_ref, sm_ref, w2_ref, b2_ref, cm_ref, nw_ref, o_ref, st_ref):
    C = CHUNK

    @pl.when(pl.program_id(0) == 0)
    def _():
        st_ref[...] = jnp.zeros_like(st_ref)

    nc = q_ref.shape[0] // C
    x = _mm(sm_ref[...], w2_ref[...]) + b2_ref[...]
    gk_all = -_softplus(-x) * (1.0 / GLA_GATE_TEMP)
    r = _iota((C, C), 0)
    c = _iota((C, C), 1)
    rcol = _iota((C, 1), 0)
    scale = GLA_K_DIM ** -0.5
    nw = nw_ref[...]
    cm = cm_ref[...]
    q_blk = q_ref[...] * scale
    k_blk = k_ref[...]

    H = range(GLA_HEADS)
    CI = range(nc)
    CH = [(ci, h) for ci in CI for h in H]
    rows = [slice(ci * C, (ci + 1) * C) for ci in CI]
    ks = [slice(h * GLA_K_DIM, (h + 1) * GLA_K_DIM) for h in H]
    vs = [slice(h * GLA_V_DIM, (h + 1) * GLA_V_DIM) for h in H]
    gk = [gk_all[rows[ci]] for ci in CI]
    yield
    cs = [_sel_left(cm, gk[ci]) for ci in CI]
    yield
    q_all = [q_blk[rows[ci]] for ci in CI]
    k_all = [k_blk[rows[ci]] for ci in CI]
    v = {(ci, h): v_ref[rows[ci], vs[h]] for ci, h in CH}
    attn = {(ci, h): jnp.where(r == c, _mm_nt(q_all[ci][:, ks[h]], k_all[ci][:, ks[h]]), 0.0)
            for ci, h in CH}
    yield
    for lvl, b in enumerate(_GLA_LEVELS):
        upper = ((rcol // b) % 2) == 1
        pair = ((r // (2 * b)) == (c // (2 * b))) & (((r // b) % 2) == 1) & (((c // b) % 2) == 0)
        if b > 1:
            base = C * (lvl + 2)
            fac = [jnp.exp(cs[ci][base:base + C]) for ci in CI]
        else:
            fac = [jnp.exp(jnp.where(upper, gk[ci], 0.0)) for ci in CI]
        qt = [jnp.where(upper, q_all[ci] * fac[ci], 0.0) for ci in CI]
        kt = [jnp.where(upper, 0.0, k_all[ci] * fac[ci]) for ci in CI]
        part = {(ci, h): _mm_nt(qt[ci][:, ks[h]], kt[ci][:, ks[h]]) for ci, h in CH}
        attn = {k_: attn[k_] + jnp.where(pair, part[k_], 0.0) for k_ in CH}
        yield
    qg = [q_all[ci] * jnp.exp(cs[ci][0:C]) for ci in CI]
    kd = [k_all[ci] * jnp.exp(cs[ci][C:2 * C]) for ci in CI]
    dec = [jnp.exp(cs[ci][C - 1:C]) for ci in CI]
    yield
    intra = {k_: _mm(attn[k_], v[k_]) for k_ in CH}
    yield
    upd = {(ci, h): _mm_tn(v[ci, h], kd[ci][:, ks[h]]) for ci, h in CH}
    yield

    st_cur = [st_ref[h] for h in H]
    out_rows = []
    for ci in CI:
        o = [_mm_nt(qg[ci][:, ks[h]], st_cur[h]) + intra[ci, h] for h in H]
        st_cur = [st_cur[h] * dec[ci][:, ks[h]] + upd[ci, h] for h in H]
        yield
        out_rows.append(jnp.concatenate(
            [(_rms(o[h], nw) * _silu(og_ref[rows[ci], vs[h]])).astype(o_ref.dtype) for h in H], axis=1))
        yield
    for h in H:
        st_ref[h] = st_cur[h]
    o_ref[...] = jnp.concatenate(out_rows, axis=0)


MIXER_SCHEDULE = "ddsg" * 24


def _mixers_kernel(dq, dk, dv, dg, sm, sz, sx, sbc, gq, gk, gv, go,
                   dn_cw, dn_par, dn_nw, s_cw, s_cb, s_par, s_dsk, s_nw, g_w2, g_b2, g_cm, g_nw,
                   o_dn, o_ssd, o_gla,
                   hq, hk, hv, dn_s, hx, hbc, ssd_st, gla_st):
    gens = {
        "d": _dn_stages(dq, dk, dv, dg, sm, dn_cw, dn_par, dn_nw, o_dn, hq, hk, hv, dn_s),
        "s": _ssd_stages(sz, sx, sbc, sm, s_cw, s_cb, s_par, s_dsk, s_nw, o_ssd, hx, hbc, ssd_st),
        "g": _gla_stages(gq, gk, gv, go, sm, g_w2, g_b2, g_cm, g_nw, o_gla, gla_st),
    }
    live = set(gens)

    def step(name):
        if name in live:
            try:
                next(gens[name])
            except StopIteration:
                live.discard(name)

    for name in MIXER_SCHEDULE:
        step(name)
    while live:
        for name in "dsg":
            step(name)


def _mixers(proj, dn_conv_w, dn_a_log, dn_dt_bias, dn_norm,
            ssm_conv_w, ssm_conv_b, ssm_dt_bias, ssm_a_log, ssm_d, ssm_norm,
            gla_w2, gla_b2, gla_norm, *, nc=MIXER_CHUNKS_PER_STEP):
    T = proj.shape[0]
    tb = nc * CHUNK
    BCW = 2 * SSM_GROUPS * SSM_STATE
    cw_total = MIX_WIDTH + BCW

    def lanes_par(lane0, n, row0, row1):
        par = jnp.zeros((8, LANES), F32)
        return par.at[0, lane0:lane0 + n].set(row0).at[1, lane0:lane0 + n].set(row1)

    dn_par = lanes_par(_SMALL_LANE[_DN_A], DN_HEADS, dn_a_log, dn_dt_bias)
    s_par = lanes_par(_SMALL_LANE[_S_DT], SSM_HEADS, ssm_a_log, ssm_dt_bias)
    dsk = jnp.repeat(ssm_d, SSM_HEAD_DIM).reshape(1, MIX_WIDTH)
    l0 = _SMALL_LANE[_G_LR]
    w2p = jnp.zeros((LANES, GLA_K_WIDTH), F32).at[l0:l0 + GLA_GATE_RANK].set(gla_w2)
    cm = jnp.asarray(_gla_cumsum_matrix(), BF16)

    blk = lambda seg, w: pl.BlockSpec((tb, w), lambda i, b=_COL[seg] // w: (i, b))
    const = lambda shape: pl.BlockSpec(shape, lambda i: (0,) * len(shape))
    out_spec = pl.BlockSpec((tb, MIX_WIDTH), lambda i: (i, 0))
    out_shape = jax.ShapeDtypeStruct((T, MIX_WIDTH), BF16)
    return pl.pallas_call(
        _mixers_kernel,
        grid=(T // tb,),
        in_specs=[blk(_DN_Q, MIX_WIDTH), blk(_DN_K, MIX_WIDTH), blk(_DN_V, MIX_WIDTH),
                  blk(_DN_G, MIX_WIDTH),
                  pl.BlockSpec((tb, LANES), lambda i: (i, _SMALL_COL // LANES)),
                  blk(_S_Z, MIX_WIDTH), blk(_S_X, MIX_WIDTH), blk(_S_B, BCW),
                  blk(_G_Q, GLA_K_WIDTH), blk(_G_K, GLA_K_WIDTH), blk(_G_V, MIX_WIDTH),
                  blk(_G_O, MIX_WIDTH),
                  const((CONV_WIDTH, 3 * MIX_WIDTH)), const((8, LANES)), const((1, DN_HEAD_DIM)),
                  const((CONV_WIDTH, cw_total)), const((1, cw_total)), const((8, LANES)),
                  const((1, MIX_WIDTH)), const((1, MIX_WIDTH)),
                  const((LANES, GLA_K_WIDTH)), const((1, GLA_K_WIDTH)), const(cm.shape),
                  const((1, GLA_V_DIM))],
        out_specs=[out_spec, out_spec, out_spec],
        out_shape=[out_shape, out_shape, out_shape],
        scratch_shapes=[pltpu.VMEM((HALO + tb, MIX_WIDTH), F32)] * 3
        + [pltpu.VMEM((DN_HEADS, DN_HEAD_DIM, DN_HEAD_DIM), F32),
           pltpu.VMEM((HALO + tb, MIX_WIDTH), F32),
           pltpu.VMEM((HALO + tb, BCW), F32),
           pltpu.VMEM((SSM_GROUPS, SSM_STATE, MIX_WIDTH // SSM_GROUPS), F32),
           pltpu.VMEM((GLA_HEADS, GLA_V_DIM, GLA_K_DIM), F32)],
        compiler_params=_cparams(("arbitrary",)),
        name="mixers",
    )(*([proj] * 12), dn_conv_w, dn_par, dn_norm.reshape(1, DN_HEAD_DIM),
      ssm_conv_w, ssm_conv_b.reshape(1, cw_total), s_par, dsk, ssm_norm.reshape(1, MIX_WIDTH),
      w2p, gla_b2.reshape(1, GLA_K_WIDTH), cm, gla_norm.reshape(1, GLA_V_DIM))


def _merge_kernel(ya_ref, yb_ref, yc_ref, ga_ref, gb_ref, gc_ref, w_ref, o_ref):
    acc = _sigmoid(ga_ref[...]) * jnp.dot(ya_ref[...], w_ref[0], preferred_element_type=F32)
    acc += _sigmoid(gb_ref[...]) * jnp.dot(yb_ref[...], w_ref[1], preferred_element_type=F32)
    acc += _sigmoid(gc_ref[...]) * jnp.dot(yc_ref[...], w_ref[2], preferred_element_type=F32)
    o_ref[...] = acc.astype(o_ref.dtype)


def _merge(y_dn, y_ssm, y_gla, proj, w_branch, li, *, tm=1024, tn=512):
    T = proj.shape[0]
    ysp = pl.BlockSpec((tm, MIX_WIDTH), lambda i, j: (i, 0))
    gate = lambda b: pl.BlockSpec(
        (tm, tn), lambda i, j, base=(_COL[_BR] + b * D_MODEL) // tn: (i, base + j))
    return pl.pallas_call(
        _merge_kernel,
        grid=(T // tm, D_MODEL // tn),
        in_specs=[ysp, ysp, ysp, gate(0), gate(1), gate(2),
                  pl.BlockSpec((None, N_BRANCH, MIX_WIDTH, tn), lambda i, j: (li, 0, 0, j))],
        out_specs=pl.BlockSpec((tm, tn), lambda i, j: (i, j)),
        out_shape=jax.ShapeDtypeStruct((T, D_MODEL), BF16),
        compiler_params=_cparams(("parallel", "arbitrary")),
        name="merge",
    )(y_dn, y_ssm, y_gla, proj, proj, proj, w_branch)


def _outproj_kernel(m_ref, x_ref, w_ref, g_ref, o_ref):
    mixed = jnp.dot(m_ref[...], w_ref[...], preferred_element_type=F32)
    o_ref[...] = x_ref[...] + _rms(mixed, g_ref[...])


def _outproj(mixed_pre, x, w_out, li, gain, *, tm=512):
    T, D = x.shape
    return pl.pallas_call(
        _outproj_kernel,
        grid=(T // tm,),
        in_specs=[pl.BlockSpec((tm, D), lambda i: (i, 0)),
                  pl.BlockSpec((tm, D), lambda i: (i, 0)),
                  pl.BlockSpec((None, D, D), lambda i: (li, 0, 0)),
                  pl.BlockSpec((1, D), lambda i: (0, 0))],
        out_specs=pl.BlockSpec((tm, D), lambda i: (i, 0)),
        out_shape=jax.ShapeDtypeStruct((T, D), F32),
        compiler_params=_cparams(("parallel",)),
        name="outproj",
    )(mixed_pre, x, w_out, gain)


def _mlp_kernel(x_ref, gpre_ref, wu_ref, wd_ref, gpost_ref, o_ref, h_ref, acc_ref):
    f = pl.program_id(1)

    @pl.when(f == 0)
    def _():
        h_ref[...] = _rms(x_ref[...], gpre_ref[...]).astype(BF16)
        acc_ref[...] = jnp.zeros_like(acc_ref)

    up = jnp.dot(h_ref[...], wu_ref[...], preferred_element_type=F32)
    act = jnp.square(jnp.maximum(up, 0.0)).astype(BF16)
    acc_ref[...] += jnp.dot(act, wd_ref[...], preferred_element_type=F32)

    @pl.when(f == pl.num_programs(1) - 1)
    def _():
        o_ref[...] = x_ref[...] + _rms(acc_ref[...], gpost_ref[...])


def _mlp(x, gpre, w_up, w_down, li, gpost, *, tm=512, tf=1024):
    T, D = x.shape
    F = w_up.shape[2]
    return pl.pallas_call(
        _mlp_kernel,
        grid=(T // tm, F // tf),
        in_specs=[pl.BlockSpec((tm, D), lambda i, f: (i, 0)),
                  pl.BlockSpec((1, D), lambda i, f: (0, 0)),
                  pl.BlockSpec((None, D, tf), lambda i, f: (li, 0, f)),
                  pl.BlockSpec((None, tf, D), lambda i, f: (li, f, 0)),
                  pl.BlockSpec((1, D), lambda i, f: (0, 0))],
        out_specs=pl.BlockSpec((tm, D), lambda i, f: (i, 0)),
        out_shape=jax.ShapeDtypeStruct((T, D), F32),
        scratch_shapes=[pltpu.VMEM((tm, D), BF16), pltpu.VMEM((tm, D), F32)],
        compiler_params=_cparams(("parallel", "arbitrary")),
        name="mlp",
    )(x, gpre, w_up, w_down, gpost)


def _ple_kernel(x_ref, p_ref, gpre_ref, wg_ref, wp_ref, gpost_ref, o_ref):
    x = x_ref[...]
    h = _rms(x, gpre_ref[...]).astype(BF16)
    gate = _sigmoid(jnp.dot(h, wg_ref[...], preferred_element_type=F32))
    e = jnp.dot(p_ref[...].astype(BF16), wp_ref[...], preferred_element_type=F32) * gate
    o_ref[...] = x + _rms(e, gpost_ref[...])


def _ple(x, p, b, gpre, w_gate, w_proj, li, gpost, *, tm=512):
    T, D = x.shape
    return pl.pallas_call(
        _ple_kernel,
        grid=(T // tm,),
        in_specs=[pl.BlockSpec((tm, D), lambda i: (i, 0)),
                  pl.BlockSpec((None, None, tm, PLE_DIM), lambda i: (li, b, i, 0)),
                  pl.BlockSpec((1, D), lambda i: (0, 0)),
                  pl.BlockSpec((None, D, D), lambda i: (li, 0, 0)),
                  pl.BlockSpec((None, PLE_DIM, D), lambda i: (li, 0, 0)),
                  pl.BlockSpec((1, D), lambda i: (0, 0))],
        out_specs=pl.BlockSpec((tm, D), lambda i: (i, 0)),
        out_shape=jax.ShapeDtypeStruct((T, D), F32),
        compiler_params=_cparams(("parallel",)),
        name="ple",
    )(x, p, gpre, w_gate, w_proj, gpost)


RELAY_TN = 512


def _relayout_row_starts():
    src = np.full((_SMALL_COL,), -1, np.int64)
    for s in _WIDE_ORDER:
        src[_COL[s]:_COL[s] + IN_SPLITS[s]] = np.arange(_IN_OFF[s], _IN_OFF[s + 1])
    starts = src[::RELAY_TN]
    for j, s0 in enumerate(starts):
        assert np.array_equal(src[j * RELAY_TN:(j + 1) * RELAY_TN], s0 + np.arange(RELAY_TN))
        assert s0 % 16 == 0
    return np.concatenate([starts, [0]]).astype(np.int32)


def _relayout_kernel(start_ref, a_ref, small_ref, o_ref):
    j = pl.program_id(1)
    last = pl.num_programs(1) - 1

    @pl.when(j < last)
    def _():
        o_ref[...] = a_ref[...].astype(BF16)

    @pl.when(j == last)
    def _():
        o_ref[...] = small_ref[...].astype(BF16)


def _relayout_w_in(w_in):
    depth, D, _ = w_in.shape
    wt = jnp.swapaxes(w_in, 1, 2)
    starts = _relayout_row_starts()
    nj = starts.shape[0]
    assert nj * RELAY_TN == IN_PAD
    small = jnp.concatenate([wt[:, _IN_OFF[s]:_IN_OFF[s + 1], :] for s in _SMALL_ORDER], axis=1)
    small = jnp.pad(small, ((0, 0), (0, RELAY_TN - small.shape[1]), (0, 0)))
    grid_spec = pltpu.PrefetchScalarGridSpec(
        num_scalar_prefetch=1,
        grid=(depth, nj),
        in_specs=[pl.BlockSpec((None, pl.Element(RELAY_TN), pl.Element(D)),
                               lambda l, j, st: (l, pl.multiple_of(st[j], 16), 0)),
                  pl.BlockSpec((None, RELAY_TN, D), lambda l, j, st: (l, 0, 0))],
        out_specs=pl.BlockSpec((None, RELAY_TN, D), lambda l, j, st: (l, j, 0)),
    )
    return pl.pallas_call(
        _relayout_kernel,
        grid_spec=grid_spec,
        out_shape=jax.ShapeDtypeStruct((depth, IN_PAD, D), BF16),
        compiler_params=_cparams(("arbitrary", "arbitrary")),
        name="relayout_w_in",
    )(jnp.asarray(starts), wt, small)


def kernel(x, p, pre_mix_norm, w_in, dn_conv_w, dn_a_log, dn_dt_bias, dn_norm, ssm_conv_w, ssm_conv_b, ssm_dt_bias, ssm_a_log, ssm_d, ssm_norm, gla_gate_w2, gla_gate_b, gla_norm, w_branch, w_out, post_mix_norm, pre_mlp_norm, w_up, w_down, post_mlp_norm, ple_pre_norm, w_ple_gate, w_ple_proj, ple_post_norm):
    Bsz, T, D = x.shape
    depth = w_in.shape[0]
    row = lambda g: g.reshape(1, -1)
    w_in16 = _relayout_w_in(w_in)
    w_branch16 = w_branch.astype(BF16)
    w_out16 = w_out.astype(BF16)
    w_up16 = w_up.astype(BF16)
    w_down16 = w_down.astype(BF16)
    w_pg16 = w_ple_gate.astype(BF16)
    w_pp16 = w_ple_proj.astype(BF16)
    outs = []
    for b in range(Bsz):
        xb = x[b]
        for i in range(depth):
            proj = _inproj(xb, row(pre_mix_norm[i]), w_in16, i)
            y_dn, y_ssm, y_gla = _mixers(
                proj, dn_conv_w[i], dn_a_log[i], dn_dt_bias[i], dn_norm[i],
                ssm_conv_w[i], ssm_conv_b[i], ssm_dt_bias[i], ssm_a_log[i], ssm_d[i], ssm_norm[i],
                gla_gate_w2[i], gla_gate_b[i], gla_norm[i])
            mixed_pre = _merge(y_dn, y_ssm, y_gla, proj, w_branch16, i)
            xb = _outproj(mixed_pre, xb, w_out16, i, row(post_mix_norm[i]))
            xb = _mlp(xb, row(pre_mlp_norm[i]), w_up16, w_down16, i, row(post_mlp_norm[i]))
            xb = _ple(xb, p, b, row(ple_pre_norm[i]), w_pg16, w_pp16, i, row(ple_post_norm[i]))
        outs.append(xb)
    return jnp.stack(outs, axis=0)
```

```python
import functools

import numpy as np
import jax
import jax.numpy as jnp
from jax import lax
from jax.experimental import pallas as pl
from jax.experimental.pallas import tpu as pltpu

F32 = jnp.float32
BF16 = jnp.bfloat16

D_MODEL = 2048
PLE_DIM = 256
NORM_EPS = 1e-6
CONV_WIDTH = 4
N_BRANCH = 3
MIX_WIDTH = D_MODEL // 2
D_FF = 4 * D_MODEL

DN_HEAD_DIM = 128
DN_HEADS = MIX_WIDTH // DN_HEAD_DIM
SSM_HEAD_DIM = 64
SSM_HEADS = MIX_WIDTH // SSM_HEAD_DIM
SSM_GROUPS = 2
SSM_STATE = 128
GLA_HEADS = 4
GLA_K_WIDTH = MIX_WIDTH // 2
GLA_K_DIM = GLA_K_WIDTH // GLA_HEADS
GLA_V_DIM = MIX_WIDTH // GLA_HEADS
GLA_GATE_RANK = 16
GLA_GATE_TEMP = 16.0

CHUNK = 64
MIXER_CHUNKS_PER_STEP = 2
LANES = 128
HALO = 8

IN_SPLITS = (
    MIX_WIDTH, MIX_WIDTH, MIX_WIDTH, DN_HEADS, DN_HEADS, MIX_WIDTH,
    MIX_WIDTH, MIX_WIDTH, SSM_GROUPS * SSM_STATE, SSM_GROUPS * SSM_STATE, SSM_HEADS,
    GLA_K_WIDTH, GLA_K_WIDTH, MIX_WIDTH, GLA_GATE_RANK, MIX_WIDTH,
    N_BRANCH * D_MODEL,
)
(_DN_Q, _DN_K, _DN_V, _DN_B, _DN_A, _DN_G, _S_Z, _S_X, _S_B, _S_C, _S_DT,
 _G_Q, _G_K, _G_V, _G_LR, _G_O, _BR) = range(17)
_IN_OFF = np.concatenate([[0], np.cumsum(IN_SPLITS)]).tolist()

_SMALL = "small"
_SMALL_SLOT = 512
_LAYOUT = (_DN_Q, _DN_K, _DN_V, _S_X, _G_V, _S_B, _S_C, _G_Q, _G_K, _SMALL, _DN_G, _S_Z, _G_O, _BR)
_SMALL_ORDER = (_DN_B, _DN_A, _S_DT, _G_LR)
_COL = {}
_off = 0
for _s in _LAYOUT:
    _COL[_s] = _off
    _off += _SMALL_SLOT if _s == _SMALL else IN_SPLITS[_s]
IN_PAD = _off
_SMALL_COL = _COL[_SMALL]
PROJ_W = _COL[_DN_G]
SG_W = _COL[_BR] - PROJ_W
BG_W = IN_PAD - _COL[_BR]
_SMALL_LANE = {}
_l = 0
for _s in _SMALL_ORDER:
    _SMALL_LANE[_s] = _l
    _l += IN_SPLITS[_s]

VMEM_LIMIT = 50 * 1024 * 1024
VMEM_LIMIT_INPROJ = 57 * 1024 * 1024


def _cparams(sem, vmem_limit=VMEM_LIMIT):
    return pltpu.CompilerParams(dimension_semantics=sem, vmem_limit_bytes=vmem_limit)


def _mm(a, b):
    return jnp.dot(a.astype(BF16), b.astype(BF16), preferred_element_type=F32)


def _mm_nt(a, b):
    return lax.dot_general(a.astype(BF16), b.astype(BF16), (((1,), (1,)), ((), ())),
                           preferred_element_type=F32)


def _mm_tn(a, b):
    return lax.dot_general(a.astype(BF16), b.astype(BF16), (((0,), (0,)), ((), ())),
                           preferred_element_type=F32)


def _split3(x):
    hi = x.astype(BF16)
    r1 = x - hi.astype(F32)
    mid = r1.astype(BF16)
    lo = (r1 - mid.astype(F32)).astype(BF16)
    return hi, mid, lo


def _sel_left(a01, x):
    n = x.shape[1]
    r = jnp.dot(a01.astype(BF16), jnp.concatenate(_split3(x), axis=1), preferred_element_type=F32)
    return (r[:, :n] + r[:, n:2 * n]) + r[:, 2 * n:]


def _sel_right(x, b01):
    m = x.shape[0]
    r = jnp.dot(jnp.concatenate(_split3(x), axis=0), b01.astype(BF16), preferred_element_type=F32)
    return (r[:m] + r[m:2 * m]) + r[2 * m:]


def _sigmoid(x):
    return 0.5 * jnp.tanh(0.5 * x) + 0.5


def _silu(x):
    return x * _sigmoid(x)


def _softplus(x):
    return jnp.maximum(x, 0.0) + jnp.log(1.0 + jnp.exp(-jnp.abs(x)))


def _rms(x, gain):
    return x * lax.rsqrt(jnp.mean(x * x, axis=-1, keepdims=True) + NORM_EPS) * gain


def _iota(shape, dim):
    return lax.broadcasted_iota(jnp.int32, shape, dim)


def _tril_ones(n):
    return (_iota((n, n), 0) >= _iota((n, n), 1)).astype(F32)


CONV_SLAB = 256


def _conv_silu(x_ref, halo_ref, w, bias):
    tb, width = x_ref.shape
    slabs = []
    for c0 in range(0, width, CONV_SLAB):
        cs = slice(c0, c0 + CONV_SLAB)
        x = x_ref[:, cs]
        halo_ref[HALO:HALO + tb, cs] = x
        y = x * w[3:4, cs]
        for k in range(CONV_WIDTH - 1):
            y = y + halo_ref[HALO - 3 + k:HALO - 3 + k + tb, cs] * w[k:k + 1, cs]
        halo_ref[0:HALO, cs] = halo_ref[tb:tb + HALO, cs]
        if bias is not None:
            y = y + bias[:, cs]
        slabs.append(_silu(y))
        yield
    return jnp.concatenate(slabs, axis=1)


INPROJ_SUB = 256


def _inproj_kernel(n_side, j_sg, j_bg, x_ref, g_ref, w_ref, *rest):
    side_in = rest[:n_side]
    o_ref, sg_ref, bg_ref = rest[n_side:n_side + 3]
    side_out = rest[n_side + 3:2 * n_side + 3]
    h_ref = rest[2 * n_side + 3]
    j = pl.program_id(1)

    @pl.when(j == 0)
    def _():
        h_ref[...] = _rms(x_ref[...], g_ref[...]).astype(BF16)

    def emit(dst_ref, epilogue):
        for c0 in range(0, w_ref.shape[0], INPROJ_SUB):
            acc = lax.dot_general(h_ref[...], w_ref[c0:c0 + INPROJ_SUB, :], (((1,), (1,)), ((), ())),
                                  preferred_element_type=F32)
            dst_ref[:, c0:c0 + INPROJ_SUB] = epilogue(acc)

    @pl.when(j < j_sg)
    def _():
        emit(o_ref, lambda a: a)

    @pl.when((j >= j_sg) & (j < j_bg))
    def _():
        emit(sg_ref, lambda a: _silu(a).astype(BF16))

    @pl.when(j >= j_bg)
    def _():
        emit(bg_ref, lambda a: _sigmoid(a).astype(BF16))

    for s_ref, d_ref in zip(side_in, side_out):
        d_ref[...] = s_ref[...].astype(BF16)


def _inproj(x, gain, wt, li, side, *, tm=1024, tn=1024):
    T, D = x.shape
    N = wt.shape[1]
    ni, nj = T // tm, N // tn
    steps = ni * nj
    j_sg, j_bg = PROJ_W // tn, (PROJ_W + SG_W) // tn
    assert PROJ_W % tn == 0 and SG_W % tn == 0 and N == IN_PAD
    bf16_rows = 16
    in_specs = [pl.BlockSpec((tm, D), lambda i, j: (i, 0)),
                pl.BlockSpec((1, D), lambda i, j: (0, 0)),
                pl.BlockSpec((None, tn, D), lambda i, j: (li, j, 0))]
    out_specs = [pl.BlockSpec((tm, tn), lambda i, j: (i, jnp.minimum(j, j_sg - 1))),
                 pl.BlockSpec((tm, tn), lambda i, j: (i, jnp.clip(j - j_sg, 0, j_bg - j_sg - 1))),
                 pl.BlockSpec((tm, tn), lambda i, j: (i, jnp.maximum(j - j_bg, 0)))]
    out_shape = [jax.ShapeDtypeStruct((T, PROJ_W), F32), jax.ShapeDtypeStruct((T, SG_W), BF16),
                 jax.ShapeDtypeStruct((T, BG_W), BF16)]
    for w in side:
        _, R, C = w.shape
        rps = -(-R // (steps * bf16_rows)) * bf16_rows
        assert R % rps == 0 and rps % bf16_rows == 0 and R // rps <= steps
        last = R // rps - 1
        in_specs.append(pl.BlockSpec(
            (None, rps, C), lambda i, j, last=last: (li, jnp.minimum(i * nj + j, last), 0)))
        out_specs.append(pl.BlockSpec(
            (None, rps, C), lambda i, j, last=last: (0, jnp.minimum(i * nj + j, last), 0)))
        out_shape.append(jax.ShapeDtypeStruct((1, R, C), BF16))
    outs = pl.pallas_call(
        functools.partial(_inproj_kernel, len(side), j_sg, j_bg),
        grid=(ni, nj),
        in_specs=in_specs,
        out_specs=out_specs,
        out_shape=out_shape,
        scratch_shapes=[pltpu.VMEM((tm, D), BF16)],
        compiler_params=_cparams(("arbitrary", "arbitrary"), VMEM_LIMIT_INPROJ),
        name="inproj",
    )(x, gain, wt, *side)
    return outs[0], outs[1], outs[2], outs[3:]


def _unit_lower_inverse_many(ms):
    n = ms[0].shape[0]
    idx = range(len(ms))
    r = _iota((n, n), 0)
    c = _iota((n, n), 1)
    same = (r // 16) == (c // 16)
    eye = (r == c).astype(F32)
    nd = [jnp.where(same, -m, 0.0) for m in ms]
    off = [jnp.where(same, 0.0, m) for m in ms]
    z = [eye + x for x in nd]
    p = [_mm(x, x) for x in nd]
    yield
    for _ in range(2):
        zp = [_mm(jnp.concatenate([z[i], p[i]], axis=0), p[i]) for i in idx]
        z = [z[i] + zp[i][:n] for i in idx]
        p = [zp[i][n:] for i in idx]
        yield
    xd = [z[i] + _mm(z[i], p[i]) for i in idx]
    yield
    n2 = [-_mm(xd[i], off[i]) for i in idx]
    yield
    z = [eye + x for x in n2]
    n4 = [_mm(n2[i], n2[i]) for i in idx]
    yield
    z = [z[i] + _mm(z[i], n4[i]) for i in idx]
    yield
    return [_mm(z[i], xd[i]) for i in idx]


def _dn_stages(q_ref, k_ref, v_ref, gate_ref, sm_ref, cw_ref, par_ref, nw_ref, o_ref,
               hq_ref, hk_ref, hv_ref, s_ref):
    C = CHUNK
    nc = q_ref.shape[0] // C

    @pl.when(pl.program_id(0) == 0)
    def _():
        hq_ref[...] = jnp.zeros_like(hq_ref)
        hk_ref[...] = jnp.zeros_like(hk_ref)
        hv_ref[...] = jnp.zeros_like(hv_ref)
        s_ref[...] = jnp.zeros_like(s_ref)

    cw = cw_ref[...]
    qc = yield from _conv_silu(q_ref, hq_ref, cw[:, 0:MIX_WIDTH], None)
    kc = yield from _conv_silu(k_ref, hk_ref, cw[:, MIX_WIDTH:2 * MIX_WIDTH], None)
    vc = yield from _conv_silu(v_ref, hv_ref, cw[:, 2 * MIX_WIDTH:3 * MIX_WIDTH], None)

    sm = sm_ref[...]
    lane = _iota((1, LANES), 1)
    a_lanes = (lane >= _SMALL_LANE[_DN_A]) & (lane < _SMALL_LANE[_DN_A] + DN_HEADS)
    a_neg = jnp.where(a_lanes, -jnp.exp(par_ref[0:1, :]), 0.0)
    beta_t = _sigmoid(sm)
    g_t = a_neg * _softplus(sm + par_ref[1:2, :])

    r = _iota((C, C), 0)
    c = _iota((C, C), 1)
    causal = r >= c
    strict = r > c
    scale = DN_HEAD_DIM ** -0.5
    nw = nw_ref[...]
    tril = _tril_ones(C)

    H = range(DN_HEADS)
    CH = [(ci, h) for ci in range(nc) for h in H]
    rows = [slice(ci * C, (ci + 1) * C) for ci in range(nc)]
    sls = [slice(h * DN_HEAD_DIM, (h + 1) * DN_HEAD_DIM) for h in H]
    bl = [_SMALL_LANE[_DN_B] + h for h in H]
    gl = [_SMALL_LANE[_DN_A] + h for h in H]

    gc_t = [_sel_left(tril, g_t[rows[ci]]) for ci in range(nc)]
    gc_rows = [jnp.concatenate([g, g], axis=0).T for g in gc_t]
    yield
    qn = [qc[:, sls[h]] for h in H]
    kn = [kc[:, sls[h]] for h in H]
    qn = [x * lax.rsqrt(jnp.sum(x * x, axis=-1, keepdims=True) + NORM_EPS) * scale for x in qn]
    yield
    kn = [x * lax.rsqrt(jnp.sum(x * x, axis=-1, keepdims=True) + NORM_EPS) for x in kn]
    yield
    qh = {(ci, h): qn[h][rows[ci]] for ci, h in CH}
    kh = {(ci, h): kn[h][rows[ci]] for ci, h in CH}
    vh = {(ci, h): vc[rows[ci], sls[h]] for ci, h in CH}
    beta = {(ci, h): beta_t[rows[ci], bl[h]:bl[h] + 1] for ci, h in CH}
    gcol = {(ci, h): gc_t[ci][:, gl[h]:gl[h] + 1] for ci, h in CH}
    grow = {(ci, h): gc_rows[ci][gl[h]:gl[h] + 1, :C] for ci, h in CH}
    glast = {k_: gcol[k_][C - 1:C, :] for k_ in CH}
    decay = {k_: jnp.where(causal, jnp.exp(gcol[k_] - grow[k_]), 0.0) for k_ in CH}
    eg = {k_: jnp.exp(gcol[k_]) for k_ in CH}
    kb = {k_: kh[k_] * beta[k_] for k_ in CH}
    yield
    kq = {k_: _mm_nt(jnp.concatenate([kb[k_], qh[k_]], axis=0), kh[k_]) for k_ in CH}
    yield
    m = [jnp.where(strict, kq[k_][:C] * decay[k_], 0.0) for k_ in CH]
    attn = {k_: kq[k_][C:] * decay[k_] for k_ in CH}
    yield
    ainv = dict(zip(CH, (yield from _unit_lower_inverse_many(m))))
    uw = {k_: _mm(ainv[k_], jnp.concatenate([vh[k_] * beta[k_], kb[k_] * eg[k_]], axis=1)) for k_ in CH}
    yield
    wq = {k_: jnp.concatenate([uw[k_][:, DN_HEAD_DIM:], qh[k_] * eg[k_]], axis=0) for k_ in CH}
    kd = {k_: kh[k_] * jnp.exp(glast[k_] - gcol[k_]) for k_ in CH}
    egl = {k_: jnp.exp(glast[k_]) for k_ in CH}
    yield

    s_cur = [s_ref[h] for h in H]
    out_rows = []
    for ci in range(nc):
        ws = [_mm(wq[ci, h], s_cur[h]) for h in H]
        yield
        v_new = [uw[ci, h][:, :DN_HEAD_DIM] - ws[h][:C] for h in H]
        o = [ws[h][C:] + _mm(attn[ci, h], v_new[h]) for h in H]
        yield
        s_cur = [s_cur[h] * egl[ci, h] + _mm_tn(kd[ci, h], v_new[h]) for h in H]
        yield
        out_rows.append(jnp.concatenate(
            [(_rms(o[h], nw) * gate_ref[rows[ci], sls[h]].astype(F32)).astype(o_ref.dtype) for h in H],
            axis=1))
        yield
    for h in H:
        s_ref[h] = s_cur[h]
    o_ref[...] = jnp.concatenate(out_rows, axis=0)


def _ssd_stages(z_ref, x_ref, bc_ref, sm_ref, cw_ref, cb_ref, par_ref, dsk_ref, nw_ref, o_ref,
                hx_ref, hbc_ref, st_ref):
    C = CHUNK
    TB = x_ref.shape[0]
    nc = TB // C
    P = SSM_HEAD_DIM
    GW = MIX_WIDTH // SSM_GROUPS
    NS = SSM_GROUPS * SSM_STATE

    @pl.when(pl.program_id(0) == 0)
    def _():
        hx_ref[...] = jnp.zeros_like(hx_ref)
        hbc_ref[...] = jnp.zeros_like(hbc_ref)
        st_ref[...] = jnp.zeros_like(st_ref)

    cw = cw_ref[...]
    cb = cb_ref[...]
    xc = yield from _conv_silu(x_ref, hx_ref, cw[:, :MIX_WIDTH], cb[:, :MIX_WIDTH])
    bcc = yield from _conv_silu(bc_ref, hbc_ref, cw[:, MIX_WIDTH:], cb[:, MIX_WIDTH:])

    sm = sm_ref[...]
    lane = _iota((1, LANES), 1)
    l0 = _SMALL_LANE[_S_DT]
    dt_lanes = (lane >= l0) & (lane < l0 + SSM_HEADS)
    a_neg = jnp.where(dt_lanes, -jnp.exp(par_ref[0:1, :]), 0.0)
    dt_t = jnp.where(dt_lanes, _softplus(sm + par_ref[1:2, :]), 0.0)
    tr = _iota((TB, TB), 0)
    tc = _iota((TB, TB), 1)
    blocktril = ((tr >= tc) & ((tr // C) == (tc // C))).astype(F32)
    acs_t = _sel_left(blocktril, dt_t * a_neg)
    yield
    er = _iota((LANES, MIX_WIDTH), 0)
    ec = _iota((LANES, MIX_WIDTH), 1)
    expand = ((er - l0) == (ec // P)).astype(F32)
    both = _sel_right(jnp.concatenate([acs_t, dt_t], axis=0), expand)
    acs_all = both[:TB]
    dt_e = both[TB:]
    rr = _iota((C, MIX_WIDTH), 0)
    cc = _iota((C, MIX_WIDTH), 1) % P
    lane128 = _iota((C, LANES), 1)
    xdt_all = xc * dt_e
    e_acs_all = jnp.exp(acs_all)
    yield

    rows = [slice(ci * C, (ci + 1) * C) for ci in range(nc)]
    CI = range(nc)
    G = range(SSM_GROUPS)
    gs = [slice(g * GW, (g + 1) * GW) for g in G]
    acs = [acs_all[rows[ci]] for ci in CI]
    xdt = [xdt_all[rows[ci]] for ci in CI]
    row_acs = [jnp.sum(jnp.where(rr == cc, acs[ci], 0.0), axis=0, keepdims=True) for ci in CI]
    last = [acs[ci][C - 1:C, :] for ci in CI]
    lmat = [jnp.where(rr >= cc, jnp.exp(acs[ci] - row_acs[ci]), 0.0) for ci in CI]
    xw = [xdt[ci] * jnp.exp(last[ci] - acs[ci]) for ci in CI]
    e_last = [jnp.exp(last[ci]) for ci in CI]
    yield
    bg = {(ci, g): bcc[rows[ci], g * SSM_STATE:(g + 1) * SSM_STATE] for ci in CI for g in G}
    cg = {(ci, g): bcc[rows[ci], NS + g * SSM_STATE:NS + (g + 1) * SSM_STATE] for ci in CI for g in G}
    cb2 = {k_: _mm_nt(cg[k_], jnp.concatenate([bg[k_], bg[k_]], axis=0)) for k_ in bg}
    upd = {(ci, g): _mm_tn(bg[ci, g], xw[ci][:, gs[g]]) for ci in CI for g in G}
    yield
    ydiag = {}
    for ci in CI:
        for g in G:
            parts = []
            for pidx in range(GW // LANES):
                col = g * GW + pidx * LANES
                wmat = cb2[ci, g] * lmat[ci][:, col:col + LANES]
                xp = xdt[ci][:, col:col + LANES]
                bd = jnp.concatenate([jnp.where(lane128 < P, xp, 0.0),
                                      jnp.where(lane128 >= P, xp, 0.0)], axis=0)
                parts.append(_mm(wmat, bd))
            ydiag[ci, g] = jnp.concatenate(parts, axis=1)
            yield

    st_cur = [st_ref[g] for g in G]
    y_rows = []
    for ci in CI:
        y_off = [_mm(cg[ci, g], st_cur[g]) * e_acs_all[rows[ci], gs[g]] for g in G]
        st_cur = [st_cur[g] * e_last[ci][:, gs[g]] + upd[ci, g] for g in G]
        y_rows.append(jnp.concatenate([ydiag[ci, g] + y_off[g] for g in G], axis=1))
        yield
    for g in G:
        st_ref[g] = st_cur[g]
    y = jnp.concatenate(y_rows, axis=0) + dsk_ref[...] * xc
    y = y * z_ref[...].astype(F32)
    yield
    nw = nw_ref[...]
    for g in G:
        o_ref[:, gs[g]] = _rms(y[:, gs[g]], nw[:, gs[g]]).astype(o_ref.dtype)


_GLA_LEVELS = (32, 16, 8, 4, 2, 1)


def _gla_cumsum_matrix():
    i = np.arange(CHUNK)[:, None]
    t = np.arange(CHUNK)[None, :]
    mats = [(t <= i), (t > i)]
    for b in _GLA_LEVELS[:-1]:
        same = (i // b) == (t // b)
        upper = ((i // b) % 2) == 1
        mats.append(np.where(upper, same & (t <= i), same & (t > i)))
    return np.concatenate(mats, axis=0).astype(np.float32)


def _gla_stages(q_ref, k_ref, v_ref, og_ref, sm_ref, w2_ref, b2_ref, cm_ref, nw_ref, o_ref, st_ref):
    C = CHUNK

    @pl.when(pl.program_id(0) == 0)
    def _():
        st_ref[...] = jnp.zeros_like(st_ref)

    nc = q_ref.shape[0] // C
    x = _mm(sm_ref[...], w2_ref[...]) + b2_ref[...]
    gk_all = -_softplus(-x) * (1.0 / GLA_GATE_TEMP)
    r = _iota((C, C), 0)
    c = _iota((C, C), 1)
    rcol = _iota((C, 1), 0)
    scale = GLA_K_DIM ** -0.5
    nw = nw_ref[...]
    cm = cm_ref[...]
    q_blk = q_ref[...] * scale
    k_blk = k_ref[...]

    H = range(GLA_HEADS)
    CI = range(nc)
    CH = [(ci, h) for ci in CI for h in H]
    rows = [slice(ci * C, (ci + 1) * C) for ci in CI]
    ks = [slice(h * GLA_K_DIM, (h + 1) * GLA_K_DIM) for h in H]
    vs = [slice(h * GLA_V_DIM, (h + 1) * GLA_V_DIM) for h in H]
    gk = [gk_all[rows[ci]] for ci in CI]
    yield
    cs = [_sel_left(cm, gk[ci]) for ci in CI]
    yield
    q_all = [q_blk[rows[ci]] for ci in CI]
    k_all = [k_blk[rows[ci]] for ci in CI]
    v = {(ci, h): v_ref[rows[ci], vs[h]] for ci, h in CH}
    attn = {(ci, h): jnp.where(r == c, _mm_nt(q_all[ci][:, ks[h]], k_all[ci][:, ks[h]]), 0.0)
            for ci, h in CH}
    yield
    for lvl, b in enumerate(_GLA_LEVELS):
        upper = ((rcol // b) % 2) == 1
        pair = ((r // (2 * b)) == (c // (2 * b))) & (((r // b) % 2) == 1) & (((c // b) % 2) == 0)
        if b > 1:
            base = C * (lvl + 2)
            fac = [jnp.exp(cs[ci][base:base + C]) for ci in CI]
        else:
            fac = [jnp.exp(jnp.where(upper, gk[ci], 0.0)) for ci in CI]
        qt = [jnp.where(upper, q_all[ci] * fac[ci], 0.0) for ci in CI]
        kt = [jnp.where(upper, 0.0, k_all[ci] * fac[ci]) for ci in CI]
        part = {(ci, h): _mm_nt(qt[ci][:, ks[h]], kt[ci][:, ks[h]]) for ci, h in CH}
        attn = {k_: attn[k_] + jnp.where(pair, part[k_], 0.0) for k_ in CH}
        yield
    qg = [q_all[ci] * jnp.exp(cs[ci][0:C]) for ci in CI]
    kd = [k_all[ci] * jnp.exp(cs[ci][C:2 * C]) for ci in CI]
    dec = [jnp.exp(cs[ci][C - 1:C]) for ci in CI]
    yield
    intra = {k_: _mm(attn[k_], v[k_]) for k_ in CH}
    yield
    upd = {(ci, h): _mm_tn(v[ci, h], kd[ci][:, ks[h]]) for ci, h in CH}
    yield

    st_cur = [st_ref[h] for h in H]
    out_rows = []
    for ci in CI:
        o = [_mm_nt(qg[ci][:, ks[h]], st_cur[h]) + intra[ci, h] for h in H]
        st_cur = [st_cur[h] * dec[ci][:, ks[h]] + upd[ci, h] for h in H]
        yield
        out_rows.append(jnp.concatenate(
            [(_rms(o[h], nw) * og_ref[rows[ci], vs[h]].astype(F32)).astype(o_ref.dtype) for h in H], axis=1))
        yield
    for h in H:
        st_ref[h] = st_cur[h]
    o_ref[...] = jnp.concatenate(out_rows, axis=0)


MIXER_SCHEDULE = "ddsg" * 24


def _mixers_kernel(dq, dk, dv, dg, sm, sz, sx, sbc, gq, gk, gv, go,
                   dn_cw, dn_par, dn_nw, s_cw, s_cb, s_par, s_dsk, s_nw, g_w2, g_b2, g_cm, g_nw,
                   o_dn, o_ssd, o_gla,
                   hq, hk, hv, dn_s, hx, hbc, ssd_st, gla_st):
    gens = {
        "d": _dn_stages(dq, dk, dv, dg, sm, dn_cw, dn_par, dn_nw, o_dn, hq, hk, hv, dn_s),
        "s": _ssd_stages(sz, sx, sbc, sm, s_cw, s_cb, s_par, s_dsk, s_nw, o_ssd, hx, hbc, ssd_st),
        "g": _gla_stages(gq, gk, gv, go, sm, g_w2, g_b2, g_cm, g_nw, o_gla, gla_st),
    }
    live = set(gens)

    def step(name):
        if name in live:
            try:
                next(gens[name])
            except StopIteration:
                live.discard(name)

    for name in MIXER_SCHEDULE:
        step(name)
    while live:
        for name in "dsg":
            step(name)


def _mixers(proj, sgate, dn_conv_w, dn_a_log, dn_dt_bias, dn_norm,
            ssm_conv_w, ssm_conv_b, ssm_dt_bias, ssm_a_log, ssm_d, ssm_norm,
            gla_w2, gla_b2, gla_norm, *, nc=MIXER_CHUNKS_PER_STEP):
    T = proj.shape[0]
    tb = nc * CHUNK
    BCW = 2 * SSM_GROUPS * SSM_STATE
    cw_total = MIX_WIDTH + BCW

    def lanes_par(lane0, n, row0, row1):
        par = jnp.zeros((8, LANES), F32)
        return par.at[0, lane0:lane0 + n].set(row0).at[1, lane0:lane0 + n].set(row1)

    dn_par = lanes_par(_SMALL_LANE[_DN_A], DN_HEADS, dn_a_log, dn_dt_bias)
    s_par = lanes_par(_SMALL_LANE[_S_DT], SSM_HEADS, ssm_a_log, ssm_dt_bias)
    dsk = jnp.repeat(ssm_d, SSM_HEAD_DIM).reshape(1, MIX_WIDTH)
    l0 = _SMALL_LANE[_G_LR]
    w2p = jnp.zeros((LANES, GLA_K_WIDTH), F32).at[l0:l0 + GLA_GATE_RANK].set(gla_w2)
    cm = jnp.asarray(_gla_cumsum_matrix(), BF16)

    blk = lambda seg, w: pl.BlockSpec((tb, w), lambda i, b=_COL[seg] // w: (i, b))
    gate = lambda seg: pl.BlockSpec((tb, MIX_WIDTH),
                                    lambda i, b=(_COL[seg] - PROJ_W) // MIX_WIDTH: (i, b))
    const = lambda shape: pl.BlockSpec(shape, lambda i: (0,) * len(shape))
    out_spec = pl.BlockSpec((tb, MIX_WIDTH), lambda i: (i, 0))
    out_shape = jax.ShapeDtypeStruct((T, MIX_WIDTH), BF16)
    return pl.pallas_call(
        _mixers_kernel,
        grid=(T // tb,),
        in_specs=[blk(_DN_Q, MIX_WIDTH), blk(_DN_K, MIX_WIDTH), blk(_DN_V, MIX_WIDTH),
                  gate(_DN_G),
                  pl.BlockSpec((tb, LANES), lambda i: (i, _SMALL_COL // LANES)),
                  gate(_S_Z), blk(_S_X, MIX_WIDTH), blk(_S_B, BCW),
                  blk(_G_Q, GLA_K_WIDTH), blk(_G_K, GLA_K_WIDTH), blk(_G_V, MIX_WIDTH),
                  gate(_G_O),
                  const((CONV_WIDTH, 3 * MIX_WIDTH)), const((8, LANES)), const((1, DN_HEAD_DIM)),
                  const((CONV_WIDTH, cw_total)), const((1, cw_total)), const((8, LANES)),
                  const((1, MIX_WIDTH)), const((1, MIX_WIDTH)),
                  const((LANES, GLA_K_WIDTH)), const((1, GLA_K_WIDTH)), const(cm.shape),
                  const((1, GLA_V_DIM))],
        out_specs=[out_spec, out_spec, out_spec],
        out_shape=[out_shape, out_shape, out_shape],
        scratch_shapes=[pltpu.VMEM((HALO + tb, MIX_WIDTH), F32)] * 3
        + [pltpu.VMEM((DN_HEADS, DN_HEAD_DIM, DN_HEAD_DIM), F32),
           pltpu.VMEM((HALO + tb, MIX_WIDTH), F32),
           pltpu.VMEM((HALO + tb, BCW), F32),
           pltpu.VMEM((SSM_GROUPS, SSM_STATE, MIX_WIDTH // SSM_GROUPS), F32),
           pltpu.VMEM((GLA_HEADS, GLA_V_DIM, GLA_K_DIM), F32)],
        compiler_params=_cparams(("arbitrary",)),
        name="mixers",
    )(proj, proj, proj, sgate, proj, sgate, proj, proj, proj, proj, proj, sgate,
      dn_conv_w, dn_par, dn_norm.reshape(1, DN_HEAD_DIM),
      ssm_conv_w, ssm_conv_b.reshape(1, cw_total), s_par, dsk, ssm_norm.reshape(1, MIX_WIDTH),
      w2p, gla_b2.reshape(1, GLA_K_WIDTH), cm, gla_norm.reshape(1, GLA_V_DIM))


def _merge_kernel(ya_ref, yb_ref, yc_ref, ga_ref, gb_ref, gc_ref, w_ref, o_ref):
    acc = ga_ref[...].astype(F32) * jnp.dot(ya_ref[...], w_ref[0], preferred_element_type=F32)
    acc += gb_ref[...].astype(F32) * jnp.dot(yb_ref[...], w_ref[1], preferred_element_type=F32)
    acc += gc_ref[...].astype(F32) * jnp.dot(yc_ref[...], w_ref[2], preferred_element_type=F32)
    o_ref[...] = acc.astype(o_ref.dtype)


def _merge(y_dn, y_ssm, y_gla, bgate, w_branch, li, *, tm=1024, tn=512):
    T = bgate.shape[0]
    ysp = pl.BlockSpec((tm, MIX_WIDTH), lambda i, j: (i, 0))
    gate = lambda b: pl.BlockSpec((tm, tn), lambda i, j, base=b * D_MODEL // tn: (i, base + j))
    return pl.pallas_call(
        _merge_kernel,
        grid=(T // tm, D_MODEL // tn),
        in_specs=[ysp, ysp, ysp, gate(0), gate(1), gate(2),
                  pl.BlockSpec((None, N_BRANCH, MIX_WIDTH, tn), lambda i, j: (li, 0, 0, j))],
        out_specs=pl.BlockSpec((tm, tn), lambda i, j: (i, j)),
        out_shape=jax.ShapeDtypeStruct((T, D_MODEL), BF16),
        compiler_params=_cparams(("parallel", "arbitrary")),
        name="merge",
    )(y_dn, y_ssm, y_gla, bgate, bgate, bgate, w_branch)


def _outproj_kernel(m_ref, x_ref, w_ref, g_ref, o_ref):
    mixed = jnp.dot(m_ref[...], w_ref[...], preferred_element_type=F32)
    o_ref[...] = x_ref[...] + _rms(mixed, g_ref[...])


def _outproj(mixed_pre, x, w_out, li, gain, *, tm=512):
    T, D = x.shape
    return pl.pallas_call(
        _outproj_kernel,
        grid=(T // tm,),
        in_specs=[pl.BlockSpec((tm, D), lambda i: (i, 0)),
                  pl.BlockSpec((tm, D), lambda i: (i, 0)),
                  pl.BlockSpec((None, D, D), lambda i: (li, 0, 0)),
                  pl.BlockSpec((1, D), lambda i: (0, 0))],
        out_specs=pl.BlockSpec((tm, D), lambda i: (i, 0)),
        out_shape=jax.ShapeDtypeStruct((T, D), F32),
        compiler_params=_cparams(("parallel",)),
        name="outproj",
    )(mixed_pre, x, w_out, gain)


def _mlp_kernel(x_ref, gpre_ref, wu_ref, wd_ref, gpost_ref, o_ref, h_ref, acc_ref):
    f = pl.program_id(1)

    @pl.when(f == 0)
    def _():
        h_ref[...] = _rms(x_ref[...], gpre_ref[...]).astype(BF16)
        acc_ref[...] = jnp.zeros_like(acc_ref)

    up = jnp.dot(h_ref[...], wu_ref[...], preferred_element_type=F32)
    act = jnp.square(jnp.maximum(up, 0.0)).astype(BF16)
    acc_ref[...] += jnp.dot(act, wd_ref[...], preferred_element_type=F32)

    @pl.when(f == pl.num_programs(1) - 1)
    def _():
        o_ref[...] = x_ref[...] + _rms(acc_ref[...], gpost_ref[...])


def _mlp(x, gpre, w_up, w_down, li, gpost, *, tm=512, tf=1024):
    T, D = x.shape
    F = w_up.shape[2]
    return pl.pallas_call(
        _mlp_kernel,
        grid=(T // tm, F // tf),
        in_specs=[pl.BlockSpec((tm, D), lambda i, f: (i, 0)),
                  pl.BlockSpec((1, D), lambda i, f: (0, 0)),
                  pl.BlockSpec((None, D, tf), lambda i, f: (li, 0, f)),
                  pl.BlockSpec((None, tf, D), lambda i, f: (li, f, 0)),
                  pl.BlockSpec((1, D), lambda i, f: (0, 0))],
        out_specs=pl.BlockSpec((tm, D), lambda i, f: (i, 0)),
        out_shape=jax.ShapeDtypeStruct((T, D), F32),
        scratch_shapes=[pltpu.VMEM((tm, D), BF16), pltpu.VMEM((tm, D), F32)],
        compiler_params=_cparams(("parallel", "arbitrary")),
        name="mlp",
    )(x, gpre, w_up, w_down, gpost)


def _ple_kernel(x_ref, p_ref, gpre_ref, wg_ref, wp_ref, gpost_ref, o_ref):
    x = x_ref[...]
    h = _rms(x, gpre_ref[...]).astype(BF16)
    gate = _sigmoid(jnp.dot(h, wg_ref[...], preferred_element_type=F32))
    e = jnp.dot(p_ref[...].astype(BF16), wp_ref[...], preferred_element_type=F32) * gate
    o_ref[...] = x + _rms(e, gpost_ref[...])


def _ple(x, p, b, gpre, w_gate, w_proj, li, gpost, p_layer, *, tm=512):
    T, D = x.shape
    return pl.pallas_call(
        _ple_kernel,
        grid=(T // tm,),
        in_specs=[pl.BlockSpec((tm, D), lambda i: (i, 0)),
                  pl.BlockSpec((None, None, tm, PLE_DIM), lambda i: (p_layer, b, i, 0)),
                  pl.BlockSpec((1, D), lambda i: (0, 0)),
                  pl.BlockSpec((None, D, D), lambda i: (li, 0, 0)),
                  pl.BlockSpec((None, PLE_DIM, D), lambda i: (li, 0, 0)),
                  pl.BlockSpec((1, D), lambda i: (0, 0))],
        out_specs=pl.BlockSpec((tm, D), lambda i: (i, 0)),
        out_shape=jax.ShapeDtypeStruct((T, D), F32),
        compiler_params=_cparams(("parallel",)),
        name="ple",
    )(x, p, gpre, w_gate, w_proj, gpost)


RELAY_TN = _SMALL_SLOT
_SMALL_TILE = _SMALL_COL // RELAY_TN


def _relayout_row_starts():
    src = np.full((IN_PAD,), -1, np.int64)
    for s in _LAYOUT:
        if s != _SMALL:
            src[_COL[s]:_COL[s] + IN_SPLITS[s]] = np.arange(_IN_OFF[s], _IN_OFF[s + 1])
    starts = src[::RELAY_TN].copy()
    assert _SMALL_COL % RELAY_TN == 0
    starts[_SMALL_TILE] = 0
    for j, s0 in enumerate(starts):
        if j != _SMALL_TILE:
            assert np.array_equal(src[j * RELAY_TN:(j + 1) * RELAY_TN], s0 + np.arange(RELAY_TN))
            assert s0 % 16 == 0
    return starts.astype(np.int32)


def _relayout_kernel(start_ref, a_ref, small_ref, o_ref):
    j = pl.program_id(1)

    @pl.when(j != _SMALL_TILE)
    def _():
        o_ref[...] = a_ref[...].astype(BF16)

    @pl.when(j == _SMALL_TILE)
    def _():
        o_ref[...] = small_ref[...].astype(BF16)


def _relayout_w_in(w_in):
    depth, D, _ = w_in.shape
    wt = jnp.swapaxes(w_in, 1, 2)
    starts = _relayout_row_starts()
    nj = starts.shape[0]
    assert nj * RELAY_TN == IN_PAD
    small = jnp.concatenate([wt[:, _IN_OFF[s]:_IN_OFF[s + 1], :] for s in _SMALL_ORDER], axis=1)
    small = jnp.pad(small, ((0, 0), (0, RELAY_TN - small.shape[1]), (0, 0)))
    grid_spec = pltpu.PrefetchScalarGridSpec(
        num_scalar_prefetch=1,
        grid=(depth, nj),
        in_specs=[pl.BlockSpec((None, pl.Element(RELAY_TN), pl.Element(D)),
                               lambda l, j, st: (l, pl.multiple_of(st[j], 16), 0)),
                  pl.BlockSpec((None, RELAY_TN, D), lambda l, j, st: (l, 0, 0))],
        out_specs=pl.BlockSpec((None, RELAY_TN, D), lambda l, j, st: (l, j, 0)),
    )
    return pl.pallas_call(
        _relayout_kernel,
        grid_spec=grid_spec,
        out_shape=jax.ShapeDtypeStruct((depth, IN_PAD, D), BF16),
        compiler_params=_cparams(("arbitrary", "arbitrary")),
        name="relayout_w_in",
    )(jnp.asarray(starts), wt, small)


def kernel(x, p, pre_mix_norm, w_in, dn_conv_w, dn_a_log, dn_dt_bias, dn_norm, ssm_conv_w, ssm_conv_b, ssm_dt_bias, ssm_a_log, ssm_d, ssm_norm, gla_gate_w2, gla_gate_b, gla_norm, w_branch, w_out, post_mix_norm, pre_mlp_norm, w_up, w_down, post_mlp_norm, ple_pre_norm, w_ple_gate, w_ple_proj, ple_post_norm):
    Bsz, T, D = x.shape
    depth = w_in.shape[0]
    row = lambda g: g.reshape(1, -1)
    w_in16 = _relayout_w_in(w_in)
    side_f32 = [w_branch.reshape(depth, N_BRANCH * MIX_WIDTH, D), w_out, w_up, w_down,
                w_ple_gate, w_ple_proj]
    cast = {}
    outs = []
    for b in range(Bsz):
        xb = x[b]
        for i in range(depth):
            proj, sgate, bgate, side = _inproj(xb, row(pre_mix_norm[i]), w_in16, i,
                                               [] if i in cast else side_f32)
            cast.setdefault(i, side)
            w_br16, w_out16, w_up16, w_down16, w_pg16, w_pp16 = cast[i]
            w_br16 = w_br16.reshape(1, N_BRANCH, MIX_WIDTH, D)
            y_dn, y_ssm, y_gla = _mixers(
                proj, sgate, dn_conv_w[i], dn_a_log[i], dn_dt_bias[i], dn_norm[i],
                ssm_conv_w[i], ssm_conv_b[i], ssm_dt_bias[i], ssm_a_log[i], ssm_d[i], ssm_norm[i],
                gla_gate_w2[i], gla_gate_b[i], gla_norm[i])
            mixed_pre = _merge(y_dn, y_ssm, y_gla, bgate, w_br16, 0)
            xb = _outproj(mixed_pre, xb, w_out16, 0, row(post_mix_norm[i]))
            xb = _mlp(xb, row(pre_mlp_norm[i]), w_up16, w_down16, 0, row(post_mlp_norm[i]))
            xb = _ple(xb, p, b, row(ple_pre_norm[i]), w_pg16, w_pp16, 0, row(ple_post_norm[i]), i)
        outs.append(xb)
    return jnp.stack(outs, axis=0)
```

```python
import functools

import numpy as np
import jax
import jax.numpy as jnp
from jax import lax
from jax.experimental import pallas as pl
from jax.experimental.pallas import tpu as pltpu

F32 = jnp.float32
BF16 = jnp.bfloat16

D_MODEL = 2048
PLE_DIM = 256
NORM_EPS = 1e-6
CONV_WIDTH = 4
N_BRANCH = 3
MIX_WIDTH = D_MODEL // 2
D_FF = 4 * D_MODEL

DN_HEAD_DIM = 128
DN_HEADS = MIX_WIDTH // DN_HEAD_DIM
SSM_HEAD_DIM = 64
SSM_HEADS = MIX_WIDTH // SSM_HEAD_DIM
SSM_GROUPS = 2
SSM_STATE = 128
GLA_HEADS = 4
GLA_K_WIDTH = MIX_WIDTH // 2
GLA_K_DIM = GLA_K_WIDTH // GLA_HEADS
GLA_V_DIM = MIX_WIDTH // GLA_HEADS
GLA_GATE_RANK = 16
GLA_GATE_TEMP = 16.0

CHUNK = 64
MIXER_CHUNKS_PER_STEP = 4
LANES = 128
HALO = 8

IN_SPLITS = (
    MIX_WIDTH, MIX_WIDTH, MIX_WIDTH, DN_HEADS, DN_HEADS, MIX_WIDTH,
    MIX_WIDTH, MIX_WIDTH, SSM_GROUPS * SSM_STATE, SSM_GROUPS * SSM_STATE, SSM_HEADS,
    GLA_K_WIDTH, GLA_K_WIDTH, MIX_WIDTH, GLA_GATE_RANK, MIX_WIDTH,
    N_BRANCH * D_MODEL,
)
(_DN_Q, _DN_K, _DN_V, _DN_B, _DN_A, _DN_G, _S_Z, _S_X, _S_B, _S_C, _S_DT,
 _G_Q, _G_K, _G_V, _G_LR, _G_O, _BR) = range(17)
_IN_OFF = np.concatenate([[0], np.cumsum(IN_SPLITS)]).tolist()

_SMALL = "small"
_SMALL_SLOT = 512
_LAYOUT = (_DN_Q, _DN_K, _DN_V, _S_X, _G_V, _S_B, _S_C, _G_Q, _G_K, _SMALL, _DN_G, _S_Z, _G_O, _BR)
_SMALL_ORDER = (_DN_B, _DN_A, _S_DT, _G_LR)
_COL = {}
_off = 0
for _s in _LAYOUT:
    _COL[_s] = _off
    _off += _SMALL_SLOT if _s == _SMALL else IN_SPLITS[_s]
IN_PAD = _off
_SMALL_COL = _COL[_SMALL]
PROJ_W = _COL[_DN_G]
SG_W = _COL[_BR] - PROJ_W
BG_W = IN_PAD - _COL[_BR]
_SMALL_LANE = {}
_l = 0
for _s in _SMALL_ORDER:
    _SMALL_LANE[_s] = _l
    _l += IN_SPLITS[_s]

VMEM_LIMIT = 50 * 1024 * 1024
VMEM_LIMIT_INPROJ = 57 * 1024 * 1024


def _cparams(sem, vmem_limit=VMEM_LIMIT):
    return pltpu.CompilerParams(dimension_semantics=sem, vmem_limit_bytes=vmem_limit)


def _mm(a, b):
    return jnp.dot(a.astype(BF16), b.astype(BF16), preferred_element_type=F32)


def _mm_nt(a, b):
    return lax.dot_general(a.astype(BF16), b.astype(BF16), (((1,), (1,)), ((), ())),
                           preferred_element_type=F32)


def _mm_tn(a, b):
    return lax.dot_general(a.astype(BF16), b.astype(BF16), (((0,), (0,)), ((), ())),
                           preferred_element_type=F32)


def _split3(x):
    hi = x.astype(BF16)
    r1 = x - hi.astype(F32)
    mid = r1.astype(BF16)
    lo = (r1 - mid.astype(F32)).astype(BF16)
    return hi, mid, lo


def _sel_left(a01, x):
    n = x.shape[1]
    r = jnp.dot(a01.astype(BF16), jnp.concatenate(_split3(x), axis=1), preferred_element_type=F32)
    return (r[:, :n] + r[:, n:2 * n]) + r[:, 2 * n:]


def _sel_right(x, b01):
    m = x.shape[0]
    r = jnp.dot(jnp.concatenate(_split3(x), axis=0), b01.astype(BF16), preferred_element_type=F32)
    return (r[:m] + r[m:2 * m]) + r[2 * m:]


def _sigmoid(x):
    return 0.5 * jnp.tanh(0.5 * x) + 0.5


def _silu(x):
    return x * _sigmoid(x)


def _softplus(x):
    return jnp.maximum(x, 0.0) + jnp.log(1.0 + jnp.exp(-jnp.abs(x)))


def _rms(x, gain):
    return x * lax.rsqrt(jnp.mean(x * x, axis=-1, keepdims=True) + NORM_EPS) * gain


def _iota(shape, dim):
    return lax.broadcasted_iota(jnp.int32, shape, dim)


def _tril_ones(n):
    return (_iota((n, n), 0) >= _iota((n, n), 1)).astype(F32)


CONV_SLAB = 256


def _conv_silu(x_ref, halo_ref, w, bias):
    tb, width = x_ref.shape
    slabs = []
    for c0 in range(0, width, CONV_SLAB):
        cs = slice(c0, c0 + CONV_SLAB)
        x = x_ref[:, cs]
        xe = jnp.concatenate([halo_ref[0:HALO, cs], x], axis=0)
        y = x * w[3:4, cs]
        for k in range(CONV_WIDTH - 1):
            y = y + xe[HALO - 3 + k:HALO - 3 + k + tb] * w[k:k + 1, cs]
        halo_ref[0:HALO, cs] = x[tb - HALO:tb]
        if bias is not None:
            y = y + bias[:, cs]
        slabs.append(_silu(y))
        yield
    return jnp.concatenate(slabs, axis=1)


INPROJ_SUB = 256


def _inproj_kernel(n_side, j_sg, j_bg, x_ref, g_ref, w_ref, *rest):
    side_in = rest[:n_side]
    o_ref, sg_ref, bg_ref = rest[n_side:n_side + 3]
    side_out = rest[n_side + 3:2 * n_side + 3]
    h_ref = rest[2 * n_side + 3]
    j = pl.program_id(1)

    @pl.when(j == 0)
    def _():
        h_ref[...] = _rms(x_ref[...], g_ref[...]).astype(BF16)

    def emit(dst_ref, epilogue):
        for c0 in range(0, w_ref.shape[0], INPROJ_SUB):
            acc = lax.dot_general(h_ref[...], w_ref[c0:c0 + INPROJ_SUB, :], (((1,), (1,)), ((), ())),
                                  preferred_element_type=F32)
            dst_ref[:, c0:c0 + INPROJ_SUB] = epilogue(acc)

    @pl.when(j < j_sg)
    def _():
        emit(o_ref, lambda a: a)

    @pl.when((j >= j_sg) & (j < j_bg))
    def _():
        emit(sg_ref, lambda a: _silu(a).astype(BF16))

    @pl.when(j >= j_bg)
    def _():
        emit(bg_ref, lambda a: _sigmoid(a).astype(BF16))

    for s_ref, d_ref in zip(side_in, side_out):
        d_ref[...] = s_ref[...].astype(BF16)


def _inproj(x, gain, wt, li, side, *, tm=1024, tn=1024):
    T, D = x.shape
    N = wt.shape[1]
    ni, nj = T // tm, N // tn
    steps = ni * nj
    j_sg, j_bg = PROJ_W // tn, (PROJ_W + SG_W) // tn
    assert PROJ_W % tn == 0 and SG_W % tn == 0 and N == IN_PAD
    bf16_rows = 16
    in_specs = [pl.BlockSpec((tm, D), lambda i, j: (i, 0)),
                pl.BlockSpec((1, D), lambda i, j: (0, 0)),
                pl.BlockSpec((None, tn, D), lambda i, j: (li, j, 0))]
    out_specs = [pl.BlockSpec((tm, tn), lambda i, j: (i, jnp.minimum(j, j_sg - 1))),
                 pl.BlockSpec((tm, tn), lambda i, j: (i, jnp.clip(j - j_sg, 0, j_bg - j_sg - 1))),
                 pl.BlockSpec((tm, tn), lambda i, j: (i, jnp.maximum(j - j_bg, 0)))]
    out_shape = [jax.ShapeDtypeStruct((T, PROJ_W), F32), jax.ShapeDtypeStruct((T, SG_W), BF16),
                 jax.ShapeDtypeStruct((T, BG_W), BF16)]
    for w in side:
        _, R, C = w.shape
        rps = -(-R // (steps * bf16_rows)) * bf16_rows
        assert R % rps == 0 and rps % bf16_rows == 0 and R // rps <= steps
        last = R // rps - 1
        in_specs.append(pl.BlockSpec(
            (None, rps, C), lambda i, j, last=last: (li, jnp.minimum(i * nj + j, last), 0)))
        out_specs.append(pl.BlockSpec(
            (None, rps, C), lambda i, j, last=last: (0, jnp.minimum(i * nj + j, last), 0)))
        out_shape.append(jax.ShapeDtypeStruct((1, R, C), BF16))
    outs = pl.pallas_call(
        functools.partial(_inproj_kernel, len(side), j_sg, j_bg),
        grid=(ni, nj),
        in_specs=in_specs,
        out_specs=out_specs,
        out_shape=out_shape,
        scratch_shapes=[pltpu.VMEM((tm, D), BF16)],
        compiler_params=_cparams(("arbitrary", "arbitrary"), VMEM_LIMIT_INPROJ),
        name="inproj",
    )(x, gain, wt, *side)
    return outs[0], outs[1], outs[2], outs[3:]


def _unit_lower_inverse_many(ms):
    n = ms[0].shape[0]
    idx = range(len(ms))
    r = _iota((n, n), 0)
    c = _iota((n, n), 1)
    same = (r // 16) == (c // 16)
    eye = (r == c).astype(F32)
    nd = [jnp.where(same, -m, 0.0) for m in ms]
    off = [jnp.where(same, 0.0, m) for m in ms]
    z = [eye + x for x in nd]
    p = [_mm(x, x) for x in nd]
    yield
    for _ in range(2):
        zp = [_mm(jnp.concatenate([z[i], p[i]], axis=0), p[i]) for i in idx]
        z = [z[i] + zp[i][:n] for i in idx]
        p = [zp[i][n:] for i in idx]
        yield
    xd = [z[i] + _mm(z[i], p[i]) for i in idx]
    yield
    n2 = [-_mm(xd[i], off[i]) for i in idx]
    yield
    z = [eye + x for x in n2]
    n4 = [_mm(n2[i], n2[i]) for i in idx]
    yield
    z = [z[i] + _mm(z[i], n4[i]) for i in idx]
    yield
    return [_mm(z[i], xd[i]) for i in idx]


def _dn_stages(q_ref, k_ref, v_ref, gate_ref, sm_ref, cw_ref, par_ref, nw_ref, o_ref,
               hq_ref, hk_ref, hv_ref, s_ref):
    C = CHUNK
    nc = q_ref.shape[0] // C

    @pl.when(pl.program_id(0) == 0)
    def _():
        hq_ref[...] = jnp.zeros_like(hq_ref)
        hk_ref[...] = jnp.zeros_like(hk_ref)
        hv_ref[...] = jnp.zeros_like(hv_ref)
        s_ref[...] = jnp.zeros_like(s_ref)

    cw = cw_ref[...]
    qc = yield from _conv_silu(q_ref, hq_ref, cw[:, 0:MIX_WIDTH], None)
    kc = yield from _conv_silu(k_ref, hk_ref, cw[:, MIX_WIDTH:2 * MIX_WIDTH], None)
    vc = yield from _conv_silu(v_ref, hv_ref, cw[:, 2 * MIX_WIDTH:3 * MIX_WIDTH], None)

    sm = sm_ref[...]
    lane = _iota((1, LANES), 1)
    a_lanes = (lane >= _SMALL_LANE[_DN_A]) & (lane < _SMALL_LANE[_DN_A] + DN_HEADS)
    a_neg = jnp.where(a_lanes, -jnp.exp(par_ref[0:1, :]), 0.0)
    beta_t = _sigmoid(sm)
    g_t = a_neg * _softplus(sm + par_ref[1:2, :])

    r = _iota((C, C), 0)
    c = _iota((C, C), 1)
    causal = r >= c
    strict = r > c
    scale = DN_HEAD_DIM ** -0.5
    nw = nw_ref[...]
    tril = _tril_ones(C)

    H = range(DN_HEADS)
    CH = [(ci, h) for ci in range(nc) for h in H]
    rows = [slice(ci * C, (ci + 1) * C) for ci in range(nc)]
    sls = [slice(h * DN_HEAD_DIM, (h + 1) * DN_HEAD_DIM) for h in H]
    bl = [_SMALL_LANE[_DN_B] + h for h in H]
    gl = [_SMALL_LANE[_DN_A] + h for h in H]

    gc_t = [_sel_left(tril, g_t[rows[ci]]) for ci in range(nc)]
    gc_rows = [jnp.concatenate([g, g], axis=0).T for g in gc_t]
    yield
    qn = [qc[:, sls[h]] for h in H]
    kn = [kc[:, sls[h]] for h in H]
    qn = [x * lax.rsqrt(jnp.sum(x * x, axis=-1, keepdims=True) + NORM_EPS) * scale for x in qn]
    yield
    kn = [x * lax.rsqrt(jnp.sum(x * x, axis=-1, keepdims=True) + NORM_EPS) for x in kn]
    yield
    qh = {(ci, h): qn[h][rows[ci]] for ci, h in CH}
    kh = {(ci, h): kn[h][rows[ci]] for ci, h in CH}
    vh = {(ci, h): vc[rows[ci], sls[h]] for ci, h in CH}
    beta = {(ci, h): beta_t[rows[ci], bl[h]:bl[h] + 1] for ci, h in CH}
    gcol = {(ci, h): gc_t[ci][:, gl[h]:gl[h] + 1] for ci, h in CH}
    grow = {(ci, h): gc_rows[ci][gl[h]:gl[h] + 1, :C] for ci, h in CH}
    glast = {k_: gcol[k_][C - 1:C, :] for k_ in CH}
    decay = {k_: jnp.where(causal, jnp.exp(gcol[k_] - grow[k_]), 0.0) for k_ in CH}
    eg = {k_: jnp.exp(gcol[k_]) for k_ in CH}
    kb = {k_: kh[k_] * beta[k_] for k_ in CH}
    yield
    kq = {k_: _mm_nt(jnp.concatenate([kb[k_], qh[k_]], axis=0), kh[k_]) for k_ in CH}
    yield
    m = [jnp.where(strict, kq[k_][:C] * decay[k_], 0.0) for k_ in CH]
    attn = {k_: kq[k_][C:] * decay[k_] for k_ in CH}
    yield
    ainv = dict(zip(CH, (yield from _unit_lower_inverse_many(m))))
    uw = {k_: _mm(ainv[k_], jnp.concatenate([vh[k_] * beta[k_], kb[k_] * eg[k_]], axis=1)) for k_ in CH}
    yield
    wq = {k_: jnp.concatenate([uw[k_][:, DN_HEAD_DIM:], qh[k_] * eg[k_]], axis=0) for k_ in CH}
    kd = {k_: kh[k_] * jnp.exp(glast[k_] - gcol[k_]) for k_ in CH}
    egl = {k_: jnp.exp(glast[k_]) for k_ in CH}
    yield

    s_cur = [s_ref[h] for h in H]
    out_rows = []
    for ci in range(nc):
        ws = [_mm(wq[ci, h], s_cur[h]) for h in H]
        yield
        v_new = [uw[ci, h][:, :DN_HEAD_DIM] - ws[h][:C] for h in H]
        o = [ws[h][C:] + _mm(attn[ci, h], v_new[h]) for h in H]
        yield
        s_cur = [s_cur[h] * egl[ci, h] + _mm_tn(kd[ci, h], v_new[h]) for h in H]
        yield
        out_rows.append(jnp.concatenate(
            [(_rms(o[h], nw) * gate_ref[rows[ci], sls[h]].astype(F32)).astype(o_ref.dtype) for h in H],
            axis=1))
        yield
    for h in H:
        s_ref[h] = s_cur[h]
    o_ref[...] = jnp.concatenate(out_rows, axis=0)


def _ssd_stages(z_ref, x_ref, bc_ref, sm_ref, cw_ref, cb_ref, par_ref, dsk_ref, nw_ref, o_ref,
                hx_ref, hbc_ref, st_ref):
    C = CHUNK
    TB = x_ref.shape[0]
    nc = TB // C
    P = SSM_HEAD_DIM
    GW = MIX_WIDTH // SSM_GROUPS
    NS = SSM_GROUPS * SSM_STATE

    @pl.when(pl.program_id(0) == 0)
    def _():
        hx_ref[...] = jnp.zeros_like(hx_ref)
        hbc_ref[...] = jnp.zeros_like(hbc_ref)
        st_ref[...] = jnp.zeros_like(st_ref)

    cw = cw_ref[...]
    cb = cb_ref[...]
    xc = yield from _conv_silu(x_ref, hx_ref, cw[:, :MIX_WIDTH], cb[:, :MIX_WIDTH])
    bcc = yield from _conv_silu(bc_ref, hbc_ref, cw[:, MIX_WIDTH:], cb[:, MIX_WIDTH:])

    sm = sm_ref[...]
    lane = _iota((1, LANES), 1)
    l0 = _SMALL_LANE[_S_DT]
    dt_lanes = (lane >= l0) & (lane < l0 + SSM_HEADS)
    a_neg = jnp.where(dt_lanes, -jnp.exp(par_ref[0:1, :]), 0.0)
    dt_t = jnp.where(dt_lanes, _softplus(sm + par_ref[1:2, :]), 0.0)
    tr = _iota((TB, TB), 0)
    tc = _iota((TB, TB), 1)
    blocktril = ((tr >= tc) & ((tr // C) == (tc // C))).astype(F32)
    acs_t = _sel_left(blocktril, dt_t * a_neg)
    yield
    er = _iota((LANES, MIX_WIDTH), 0)
    ec = _iota((LANES, MIX_WIDTH), 1)
    expand = ((er - l0) == (ec // P)).astype(F32)
    both = _sel_right(jnp.concatenate([acs_t, dt_t], axis=0), expand)
    acs_all = both[:TB]
    dt_e = both[TB:]
    rr = _iota((C, MIX_WIDTH), 0)
    cc = _iota((C, MIX_WIDTH), 1) % P
    lane128 = _iota((C, LANES), 1)
    xdt_all = xc * dt_e
    e_acs_all = jnp.exp(acs_all)
    yield

    rows = [slice(ci * C, (ci + 1) * C) for ci in range(nc)]
    CI = range(nc)
    G = range(SSM_GROUPS)
    gs = [slice(g * GW, (g + 1) * GW) for g in G]
    acs = [acs_all[rows[ci]] for ci in CI]
    xdt = [xdt_all[rows[ci]] for ci in CI]
    row_acs = [jnp.sum(jnp.where(rr == cc, acs[ci], 0.0), axis=0, keepdims=True) for ci in CI]
    last = [acs[ci][C - 1:C, :] for ci in CI]
    lmat = [jnp.where(rr >= cc, jnp.exp(acs[ci] - row_acs[ci]), 0.0) for ci in CI]
    xw = [xdt[ci] * jnp.exp(last[ci] - acs[ci]) for ci in CI]
    e_last = [jnp.exp(last[ci]) for ci in CI]
    yield
    bg = {(ci, g): bcc[rows[ci], g * SSM_STATE:(g + 1) * SSM_STATE] for ci in CI for g in G}
    cg = {(ci, g): bcc[rows[ci], NS + g * SSM_STATE:NS + (g + 1) * SSM_STATE] for ci in CI for g in G}
    cb2 = {k_: _mm_nt(cg[k_], jnp.concatenate([bg[k_], bg[k_]], axis=0)) for k_ in bg}
    upd = {(ci, g): _mm_tn(bg[ci, g], xw[ci][:, gs[g]]) for ci in CI for g in G}
    yield
    ydiag = {}
    for ci in CI:
        for g in G:
            parts = []
            for pidx in range(GW // LANES):
                col = g * GW + pidx * LANES
                wmat = cb2[ci, g] * lmat[ci][:, col:col + LANES]
                xp = xdt[ci][:, col:col + LANES]
                bd = jnp.concatenate([jnp.where(lane128 < P, xp, 0.0),
                                      jnp.where(lane128 >= P, xp, 0.0)], axis=0)
                parts.append(_mm(wmat, bd))
            ydiag[ci, g] = jnp.concatenate(parts, axis=1)
            yield

    st_cur = [st_ref[g] for g in G]
    y_rows = []
    for ci in CI:
        y_off = [_mm(cg[ci, g], st_cur[g]) * e_acs_all[rows[ci], gs[g]] for g in G]
        st_cur = [st_cur[g] * e_last[ci][:, gs[g]] + upd[ci, g] for g in G]
        y_rows.append(jnp.concatenate([ydiag[ci, g] + y_off[g] for g in G], axis=1))
        yield
    for g in G:
        st_ref[g] = st_cur[g]
    y = jnp.concatenate(y_rows, axis=0) + dsk_ref[...] * xc
    y = y * z_ref[...].astype(F32)
    yield
    nw = nw_ref[...]
    for g in G:
        o_ref[:, gs[g]] = _rms(y[:, gs[g]], nw[:, gs[g]]).astype(o_ref.dtype)


_GLA_LEVELS = (32, 16, 8, 4, 2, 1)


def _gla_cumsum_matrix():
    i = np.arange(CHUNK)[:, None]
    t = np.arange(CHUNK)[None, :]
    mats = [(t <= i), (t > i)]
    for b in _GLA_LEVELS[:-1]:
        same = (i // b) == (t // b)
        upper = ((i // b) % 2) == 1
        mats.append(np.where(upper, same & (t <= i), same & (t > i)))
    return np.concatenate(mats, axis=0).astype(np.float32)


def _gla_stages(q_ref, k_ref, v_ref, og_ref, sm_ref, w2_ref, b2_ref, cm_ref, nw_ref, o_ref, st_ref):
    C = CHUNK

    @pl.when(pl.program_id(0) == 0)
    def _():
        st_ref[...] = jnp.zeros_like(st_ref)

    nc = q_ref.shape[0] // C
    x = _mm(sm_ref[...], w2_ref[...]) + b2_ref[...]
    gk_all = -_softplus(-x) * (1.0 / GLA_GATE_TEMP)
    r = _iota((C, C), 0)
    c = _iota((C, C), 1)
    rcol = _iota((C, 1), 0)
    scale = GLA_K_DIM ** -0.5
    nw = nw_ref[...]
    cm = cm_ref[...]
    q_blk = q_ref[...] * scale
    k_blk = k_ref[...]

    H = range(GLA_HEADS)
    CI = range(nc)
    CH = [(ci, h) for ci in CI for h in H]
    rows = [slice(ci * C, (ci + 1) * C) for ci in CI]
    ks = [slice(h * GLA_K_DIM, (h + 1) * GLA_K_DIM) for h in H]
    vs = [slice(h * GLA_V_DIM, (h + 1) * GLA_V_DIM) for h in H]
    gk = [gk_all[rows[ci]] for ci in CI]
    yield
    cs = [_sel_left(cm, gk[ci]) for ci in CI]
    yield
    q_all = [q_blk[rows[ci]] for ci in CI]
    k_all = [k_blk[rows[ci]] for ci in CI]
    v = {(ci, h): v_ref[rows[ci], vs[h]] for ci, h in CH}
    attn = {(ci, h): jnp.where(r == c, _mm_nt(q_all[ci][:, ks[h]], k_all[ci][:, ks[h]]), 0.0)
            for ci, h in CH}
    yield
    for lvl, b in enumerate(_GLA_LEVELS):
        upper = ((rcol // b) % 2) == 1
        pair = ((r // (2 * b)) == (c // (2 * b))) & (((r // b) % 2) == 1) & (((c // b) % 2) == 0)
        if b > 1:
            base = C * (lvl + 2)
            fac = [jnp.exp(cs[ci][base:base + C]) for ci in CI]
        else:
            fac = [jnp.exp(jnp.where(upper, gk[ci], 0.0)) for ci in CI]
        qt = [jnp.where(upper, q_all[ci] * fac[ci], 0.0) for ci in CI]
        kt = [jnp.where(upper, 0.0, k_all[ci] * fac[ci]) for ci in CI]
        part = {(ci, h): _mm_nt(qt[ci][:, ks[h]], kt[ci][:, ks[h]]) for ci, h in CH}
        attn = {k_: attn[k_] + jnp.where(pair, part[k_], 0.0) for k_ in CH}
        yield
    qg = [q_all[ci] * jnp.exp(cs[ci][0:C]) for ci in CI]
    kd = [k_all[ci] * jnp.exp(cs[ci][C:2 * C]) for ci in CI]
    dec = [jnp.exp(cs[ci][C - 1:C]) for ci in CI]
    yield
    intra = {k_: _mm(attn[k_], v[k_]) for k_ in CH}
    yield
    upd = {(ci, h): _mm_tn(v[ci, h], kd[ci][:, ks[h]]) for ci, h in CH}
    yield

    st_cur = [st_ref[h] for h in H]
    out_rows = []
    for ci in CI:
        o = [_mm_nt(qg[ci][:, ks[h]], st_cur[h]) + intra[ci, h] for h in H]
        st_cur = [st_cur[h] * dec[ci][:, ks[h]] + upd[ci, h] for h in H]
        yield
        out_rows.append(jnp.concatenate(
            [(_rms(o[h], nw) * og_ref[rows[ci], vs[h]].astype(F32)).astype(o_ref.dtype) for h in H], axis=1))
        yield
    for h in H:
        st_ref[h] = st_cur[h]
    o_ref[...] = jnp.concatenate(out_rows, axis=0)


MIXER_SCHEDULE = "ddsg" * 24


def _mixers_kernel(dq, dk, dv, dg, sm, sz, sx, sbc, gq, gk, gv, go,
                   dn_cw, dn_par, dn_nw, s_cw, s_cb, s_par, s_dsk, s_nw, g_w2, g_b2, g_cm, g_nw,
                   o_dn, o_ssd, o_gla,
                   hq, hk, hv, dn_s, hx, hbc, ssd_st, gla_st):
    gens = {
        "d": _dn_stages(dq, dk, dv, dg, sm, dn_cw, dn_par, dn_nw, o_dn, hq, hk, hv, dn_s),
        "s": _ssd_stages(sz, sx, sbc, sm, s_cw, s_cb, s_par, s_dsk, s_nw, o_ssd, hx, hbc, ssd_st),
        "g": _gla_stages(gq, gk, gv, go, sm, g_w2, g_b2, g_cm, g_nw, o_gla, gla_st),
    }
    live = set(gens)

    def step(name):
        if name in live:
            try:
                next(gens[name])
            except StopIteration:
                live.discard(name)

    for name in MIXER_SCHEDULE:
        step(name)
    while live:
        for name in "dsg":
            step(name)


def _mixers(proj, sgate, dn_conv_w, dn_a_log, dn_dt_bias, dn_norm,
            ssm_conv_w, ssm_conv_b, ssm_dt_bias, ssm_a_log, ssm_d, ssm_norm,
            gla_w2, gla_b2, gla_norm, *, nc=MIXER_CHUNKS_PER_STEP):
    T = proj.shape[0]
    tb = nc * CHUNK
    BCW = 2 * SSM_GROUPS * SSM_STATE
    cw_total = MIX_WIDTH + BCW

    def lanes_par(lane0, n, row0, row1):
        par = jnp.zeros((8, LANES), F32)
        return par.at[0, lane0:lane0 + n].set(row0).at[1, lane0:lane0 + n].set(row1)

    dn_par = lanes_par(_SMALL_LANE[_DN_A], DN_HEADS, dn_a_log, dn_dt_bias)
    s_par = lanes_par(_SMALL_LANE[_S_DT], SSM_HEADS, ssm_a_log, ssm_dt_bias)
    dsk = jnp.repeat(ssm_d, SSM_HEAD_DIM).reshape(1, MIX_WIDTH)
    l0 = _SMALL_LANE[_G_LR]
    w2p = jnp.zeros((LANES, GLA_K_WIDTH), F32).at[l0:l0 + GLA_GATE_RANK].set(gla_w2)
    cm = jnp.asarray(_gla_cumsum_matrix(), BF16)

    blk = lambda seg, w: pl.BlockSpec((tb, w), lambda i, b=_COL[seg] // w: (i, b))
    gate = lambda seg: pl.BlockSpec((tb, MIX_WIDTH),
                                    lambda i, b=(_COL[seg] - PROJ_W) // MIX_WIDTH: (i, b))
    const = lambda shape: pl.BlockSpec(shape, lambda i: (0,) * len(shape))
    out_spec = pl.BlockSpec((tb, MIX_WIDTH), lambda i: (i, 0))
    out_shape = jax.ShapeDtypeStruct((T, MIX_WIDTH), BF16)
    return pl.pallas_call(
        _mixers_kernel,
        grid=(T // tb,),
        in_specs=[blk(_DN_Q, MIX_WIDTH), blk(_DN_K, MIX_WIDTH), blk(_DN_V, MIX_WIDTH),
                  gate(_DN_G),
                  pl.BlockSpec((tb, LANES), lambda i: (i, _SMALL_COL // LANES)),
                  gate(_S_Z), blk(_S_X, MIX_WIDTH), blk(_S_B, BCW),
                  blk(_G_Q, GLA_K_WIDTH), blk(_G_K, GLA_K_WIDTH), blk(_G_V, MIX_WIDTH),
                  gate(_G_O),
                  const((CONV_WIDTH, 3 * MIX_WIDTH)), const((8, LANES)), const((1, DN_HEAD_DIM)),
                  const((CONV_WIDTH, cw_total)), const((1, cw_total)), const((8, LANES)),
                  const((1, MIX_WIDTH)), const((1, MIX_WIDTH)),
                  const((LANES, GLA_K_WIDTH)), const((1, GLA_K_WIDTH)), const(cm.shape),
                  const((1, GLA_V_DIM))],
        out_specs=[out_spec, out_spec, out_spec],
        out_shape=[out_shape, out_shape, out_shape],
        scratch_shapes=[pltpu.VMEM((HALO, MIX_WIDTH), F32)] * 3
        + [pltpu.VMEM((DN_HEADS, DN_HEAD_DIM, DN_HEAD_DIM), F32),
           pltpu.VMEM((HALO, MIX_WIDTH), F32),
           pltpu.VMEM((HALO, BCW), F32),
           pltpu.VMEM((SSM_GROUPS, SSM_STATE, MIX_WIDTH // SSM_GROUPS), F32),
           pltpu.VMEM((GLA_HEADS, GLA_V_DIM, GLA_K_DIM), F32)],
        compiler_params=_cparams(("arbitrary",)),
        name="mixers",
    )(proj, proj, proj, sgate, proj, sgate, proj, proj, proj, proj, proj, sgate,
      dn_conv_w, dn_par, dn_norm.reshape(1, DN_HEAD_DIM),
      ssm_conv_w, ssm_conv_b.reshape(1, cw_total), s_par, dsk, ssm_norm.reshape(1, MIX_WIDTH),
      w2p, gla_b2.reshape(1, GLA_K_WIDTH), cm, gla_norm.reshape(1, GLA_V_DIM))


def _merge_kernel(ya_ref, yb_ref, yc_ref, ga_ref, gb_ref, gc_ref, w_ref, o_ref):
    acc = ga_ref[...].astype(F32) * jnp.dot(ya_ref[...], w_ref[0], preferred_element_type=F32)
    acc += gb_ref[...].astype(F32) * jnp.dot(yb_ref[...], w_ref[1], preferred_element_type=F32)
    acc += gc_ref[...].astype(F32) * jnp.dot(yc_ref[...], w_ref[2], preferred_element_type=F32)
    o_ref[...] = acc.astype(o_ref.dtype)


def _merge(y_dn, y_ssm, y_gla, bgate, w_branch, li, *, tm=1024, tn=1024):
    T = bgate.shape[0]
    ysp = pl.BlockSpec((tm, MIX_WIDTH), lambda i, j: (i, 0))
    gate = lambda b: pl.BlockSpec((tm, tn), lambda i, j, base=b * D_MODEL // tn: (i, base + j))
    return pl.pallas_call(
        _merge_kernel,
        grid=(T // tm, D_MODEL // tn),
        in_specs=[ysp, ysp, ysp, gate(0), gate(1), gate(2),
                  pl.BlockSpec((None, N_BRANCH, MIX_WIDTH, tn), lambda i, j: (li, 0, 0, j))],
        out_specs=pl.BlockSpec((tm, tn), lambda i, j: (i, j)),
        out_shape=jax.ShapeDtypeStruct((T, D_MODEL), BF16),
        compiler_params=_cparams(("parallel", "arbitrary")),
        name="merge",
    )(y_dn, y_ssm, y_gla, bgate, bgate, bgate, w_branch)


def _outproj_kernel(m_ref, x_ref, w_ref, g_ref, o_ref):
    mixed = jnp.dot(m_ref[...], w_ref[...], preferred_element_type=F32)
    o_ref[...] = x_ref[...] + _rms(mixed, g_ref[...])


def _outproj(mixed_pre, x, w_out, li, gain, *, tm=512):
    T, D = x.shape
    return pl.pallas_call(
        _outproj_kernel,
        grid=(T // tm,),
        in_specs=[pl.BlockSpec((tm, D), lambda i: (i, 0)),
                  pl.BlockSpec((tm, D), lambda i: (i, 0)),
                  pl.BlockSpec((None, D, D), lambda i: (li, 0, 0)),
                  pl.BlockSpec((1, D), lambda i: (0, 0))],
        out_specs=pl.BlockSpec((tm, D), lambda i: (i, 0)),
        out_shape=jax.ShapeDtypeStruct((T, D), F32),
        compiler_params=_cparams(("parallel",)),
        name="outproj",
    )(mixed_pre, x, w_out, gain)


def _mlp_kernel(x_ref, gpre_ref, wu_ref, wd_ref, gpost_ref, o_ref, h_ref, acc_ref):
    f = pl.program_id(1)

    @pl.when(f == 0)
    def _():
        h_ref[...] = _rms(x_ref[...], gpre_ref[...]).astype(BF16)
        acc_ref[...] = jnp.zeros_like(acc_ref)

    up = jnp.dot(h_ref[...], wu_ref[...], preferred_element_type=F32)
    act = jnp.square(jnp.maximum(up, 0.0)).astype(BF16)
    acc_ref[...] += jnp.dot(act, wd_ref[...], preferred_element_type=F32)

    @pl.when(f == pl.num_programs(1) - 1)
    def _():
        o_ref[...] = x_ref[...] + _rms(acc_ref[...], gpost_ref[...])


def _mlp(x, gpre, w_up, w_down, li, gpost, *, tm=512, tf=1024):
    T, D = x.shape
    F = w_up.shape[2]
    return pl.pallas_call(
        _mlp_kernel,
        grid=(T // tm, F // tf),
        in_specs=[pl.BlockSpec((tm, D), lambda i, f: (i, 0)),
                  pl.BlockSpec((1, D), lambda i, f: (0, 0)),
                  pl.BlockSpec((None, D, tf), lambda i, f: (li, 0, f)),
                  pl.BlockSpec((None, tf, D), lambda i, f: (li, f, 0)),
                  pl.BlockSpec((1, D), lambda i, f: (0, 0))],
        out_specs=pl.BlockSpec((tm, D), lambda i, f: (i, 0)),
        out_shape=jax.ShapeDtypeStruct((T, D), F32),
        scratch_shapes=[pltpu.VMEM((tm, D), BF16), pltpu.VMEM((tm, D), F32)],
        compiler_params=_cparams(("parallel", "arbitrary")),
        name="mlp",
    )(x, gpre, w_up, w_down, gpost)


def _ple_kernel(x_ref, p_ref, gpre_ref, wg_ref, wp_ref, gpost_ref, o_ref):
    x = x_ref[...]
    h = _rms(x, gpre_ref[...]).astype(BF16)
    gate = _sigmoid(jnp.dot(h, wg_ref[...], preferred_element_type=F32))
    e = jnp.dot(p_ref[...].astype(BF16), wp_ref[...], preferred_element_type=F32) * gate
    o_ref[...] = x + _rms(e, gpost_ref[...])


def _ple(x, p, b, gpre, w_gate, w_proj, li, gpost, p_layer, *, tm=512):
    T, D = x.shape
    return pl.pallas_call(
        _ple_kernel,
        grid=(T // tm,),
        in_specs=[pl.BlockSpec((tm, D), lambda i: (i, 0)),
                  pl.BlockSpec((None, None, tm, PLE_DIM), lambda i: (p_layer, b, i, 0)),
                  pl.BlockSpec((1, D), lambda i: (0, 0)),
                  pl.BlockSpec((None, D, D), lambda i: (li, 0, 0)),
                  pl.BlockSpec((None, PLE_DIM, D), lambda i: (li, 0, 0)),
                  pl.BlockSpec((1, D), lambda i: (0, 0))],
        out_specs=pl.BlockSpec((tm, D), lambda i: (i, 0)),
        out_shape=jax.ShapeDtypeStruct((T, D), F32),
        compiler_params=_cparams(("parallel",)),
        name="ple",
    )(x, p, gpre, w_gate, w_proj, gpost)


RELAY_TN = _SMALL_SLOT
_SMALL_TILE = _SMALL_COL // RELAY_TN


def _relayout_row_starts():
    src = np.full((IN_PAD,), -1, np.int64)
    for s in _LAYOUT:
        if s != _SMALL:
            src[_COL[s]:_COL[s] + IN_SPLITS[s]] = np.arange(_IN_OFF[s], _IN_OFF[s + 1])
    starts = src[::RELAY_TN].copy()
    assert _SMALL_COL % RELAY_TN == 0
    starts[_SMALL_TILE] = 0
    for j, s0 in enumerate(starts):
        if j != _SMALL_TILE:
            assert np.array_equal(src[j * RELAY_TN:(j + 1) * RELAY_TN], s0 + np.arange(RELAY_TN))
            assert s0 % 16 == 0
    return starts.astype(np.int32)


def _relayout_kernel(start_ref, a_ref, small_ref, o_ref):
    j = pl.program_id(1)

    @pl.when(j != _SMALL_TILE)
    def _():
        o_ref[...] = a_ref[...].astype(BF16)

    @pl.when(j == _SMALL_TILE)
    def _():
        o_ref[...] = small_ref[...].astype(BF16)


def _relayout_w_in(w_in):
    depth, D, _ = w_in.shape
    wt = jnp.swapaxes(w_in, 1, 2)
    starts = _relayout_row_starts()
    nj = starts.shape[0]
    assert nj * RELAY_TN == IN_PAD
    small = jnp.concatenate([wt[:, _IN_OFF[s]:_IN_OFF[s + 1], :] for s in _SMALL_ORDER], axis=1)
    small = jnp.pad(small, ((0, 0), (0, RELAY_TN - small.shape[1]), (0, 0)))
    grid_spec = pltpu.PrefetchScalarGridSpec(
        num_scalar_prefetch=1,
        grid=(depth, nj),
        in_specs=[pl.BlockSpec((None, pl.Element(RELAY_TN), pl.Element(D)),
                               lambda l, j, st: (l, pl.multiple_of(st[j], 16), 0)),
                  pl.BlockSpec((None, RELAY_TN, D), lambda l, j, st: (l, 0, 0))],
        out_specs=pl.BlockSpec((None, RELAY_TN, D), lambda l, j, st: (l, j, 0)),
    )
    return pl.pallas_call(
        _relayout_kernel,
        grid_spec=grid_spec,
        out_shape=jax.ShapeDtypeStruct((depth, IN_PAD, D), BF16),
        compiler_params=_cparams(("arbitrary", "arbitrary")),
        name="relayout_w_in",
    )(jnp.asarray(starts), wt, small)


def kernel(x, p, pre_mix_norm, w_in, dn_conv_w, dn_a_log, dn_dt_bias, dn_norm, ssm_conv_w, ssm_conv_b, ssm_dt_bias, ssm_a_log, ssm_d, ssm_norm, gla_gate_w2, gla_gate_b, gla_norm, w_branch, w_out, post_mix_norm, pre_mlp_norm, w_up, w_down, post_mlp_norm, ple_pre_norm, w_ple_gate, w_ple_proj, ple_post_norm):
    Bsz, T, D = x.shape
    depth = w_in.shape[0]
    row = lambda g: g.reshape(1, -1)
    w_in16 = _relayout_w_in(w_in)
    side_f32 = [w_branch.reshape(depth, N_BRANCH * MIX_WIDTH, D), w_out, w_up, w_down,
                w_ple_gate, w_ple_proj]
    cast = {}
    outs = []
    for b in range(Bsz):
        xb = x[b]
        for i in range(depth):
            proj, sgate, bgate, side = _inproj(xb, row(pre_mix_norm[i]), w_in16, i,
                                               [] if i in cast else side_f32)
            cast.setdefault(i, side)
            w_br16, w_out16, w_up16, w_down16, w_pg16, w_pp16 = cast[i]
            w_br16 = w_br16.reshape(1, N_BRANCH, MIX_WIDTH, D)
            y_dn, y_ssm, y_gla = _mixers(
                proj, sgate, dn_conv_w[i], dn_a_log[i], dn_dt_bias[i], dn_norm[i],
                ssm_conv_w[i], ssm_conv_b[i], ssm_dt_bias[i], ssm_a_log[i], ssm_d[i], ssm_norm[i],
                gla_gate_w2[i], gla_gate_b[i], gla_norm[i])
            mixed_pre = _merge(y_dn, y_ssm, y_gla, bgate, w_br16, 0)
            xb = _outproj(mixed_pre, xb, w_out16, 0, row(post_mix_norm[i]))
            xb = _mlp(xb, row(pre_mlp_norm[i]), w_up16, w_down16, 0, row(post_mlp_norm[i]))
            xb = _ple(xb, p, b, row(ple_pre_norm[i]), w_pg16, w_pp16, 0, row(ple_post_norm[i]), i)
        outs.append(xb)
    return jnp.stack(outs, axis=0)
```

```python
import functools

import numpy as np
import jax
import jax.numpy as jnp
from jax import lax
from jax.experimental import pallas as pl
from jax.experimental.pallas import tpu as pltpu

F32 = jnp.float32
BF16 = jnp.bfloat16

D_MODEL = 2048
PLE_DIM = 256
NORM_EPS = 1e-6
CONV_WIDTH = 4
N_BRANCH = 3
MIX_WIDTH = D_MODEL // 2
D_FF = 4 * D_MODEL

DN_HEAD_DIM = 128
DN_HEADS = MIX_WIDTH // DN_HEAD_DIM
SSM_HEAD_DIM = 64
SSM_HEADS = MIX_WIDTH // SSM_HEAD_DIM
SSM_GROUPS = 2
SSM_STATE = 128
GLA_HEADS = 4
GLA_K_WIDTH = MIX_WIDTH // 2
GLA_K_DIM = GLA_K_WIDTH // GLA_HEADS
GLA_V_DIM = MIX_WIDTH // GLA_HEADS
GLA_GATE_RANK = 16
GLA_GATE_TEMP = 16.0

CHUNK = 64
MIXER_CHUNKS_PER_STEP = 4
LANES = 128
HALO = 8

IN_SPLITS = (
    MIX_WIDTH, MIX_WIDTH, MIX_WIDTH, DN_HEADS, DN_HEADS, MIX_WIDTH,
    MIX_WIDTH, MIX_WIDTH, SSM_GROUPS * SSM_STATE, SSM_GROUPS * SSM_STATE, SSM_HEADS,
    GLA_K_WIDTH, GLA_K_WIDTH, MIX_WIDTH, GLA_GATE_RANK, MIX_WIDTH,
    N_BRANCH * D_MODEL,
)
(_DN_Q, _DN_K, _DN_V, _DN_B, _DN_A, _DN_G, _S_Z, _S_X, _S_B, _S_C, _S_DT,
 _G_Q, _G_K, _G_V, _G_LR, _G_O, _BR) = range(17)
_IN_OFF = np.concatenate([[0], np.cumsum(IN_SPLITS)]).tolist()

_SMALL = "small"
_SMALL_SLOT = 512
_LAYOUT = (_DN_Q, _DN_K, _DN_V, _S_X, _G_V, _S_B, _S_C, _G_Q, _G_K, _SMALL, _DN_G, _S_Z, _G_O, _BR)
_SMALL_ORDER = (_DN_B, _DN_A, _S_DT, _G_LR)
_COL = {}
_off = 0
for _s in _LAYOUT:
    _COL[_s] = _off
    _off += _SMALL_SLOT if _s == _SMALL else IN_SPLITS[_s]
IN_PAD = _off
_SMALL_COL = _COL[_SMALL]
PROJ_W = _COL[_DN_G]
SG_W = _COL[_BR] - PROJ_W
BG_W = IN_PAD - _COL[_BR]
_SMALL_LANE = {}
_l = 0
for _s in _SMALL_ORDER:
    _SMALL_LANE[_s] = _l
    _l += IN_SPLITS[_s]

VMEM_LIMIT = 50 * 1024 * 1024
VMEM_LIMIT_INPROJ = 57 * 1024 * 1024


def _cparams(sem, vmem_limit=VMEM_LIMIT):
    return pltpu.CompilerParams(dimension_semantics=sem, vmem_limit_bytes=vmem_limit)


def _mm(a, b):
    return jnp.dot(a.astype(BF16), b.astype(BF16), preferred_element_type=F32)


def _mm_nt(a, b):
    return lax.dot_general(a.astype(BF16), b.astype(BF16), (((1,), (1,)), ((), ())),
                           preferred_element_type=F32)


def _mm_tn(a, b):
    return lax.dot_general(a.astype(BF16), b.astype(BF16), (((0,), (0,)), ((), ())),
                           preferred_element_type=F32)


def _split3(x):
    hi = x.astype(BF16)
    r1 = x - hi.astype(F32)
    mid = r1.astype(BF16)
    lo = (r1 - mid.astype(F32)).astype(BF16)
    return hi, mid, lo


def _sel_left(a01, x):
    n = x.shape[1]
    r = jnp.dot(a01.astype(BF16), jnp.concatenate(_split3(x), axis=1), preferred_element_type=F32)
    return (r[:, :n] + r[:, n:2 * n]) + r[:, 2 * n:]


def _sel_right(x, b01):
    m = x.shape[0]
    r = jnp.dot(jnp.concatenate(_split3(x), axis=0), b01.astype(BF16), preferred_element_type=F32)
    return (r[:m] + r[m:2 * m]) + r[2 * m:]


def _sigmoid(x):
    return 0.5 * jnp.tanh(0.5 * x) + 0.5


def _silu(x):
    return x * _sigmoid(x)


def _softplus(x):
    return jnp.maximum(x, 0.0) + jnp.log(1.0 + jnp.exp(-jnp.abs(x)))


def _rms(x, gain):
    return x * lax.rsqrt(jnp.mean(x * x, axis=-1, keepdims=True) + NORM_EPS) * gain


def _iota(shape, dim):
    return lax.broadcasted_iota(jnp.int32, shape, dim)


def _tril_ones(n):
    return (_iota((n, n), 0) >= _iota((n, n), 1)).astype(F32)


CONV_SLAB = 256


def _conv_silu(x_ref, halo_ref, w, bias):
    tb, width = x_ref.shape
    slabs = []
    for c0 in range(0, width, CONV_SLAB):
        cs = slice(c0, c0 + CONV_SLAB)
        x = x_ref[:, cs]
        xe = jnp.concatenate([halo_ref[0:HALO, cs], x], axis=0)
        y = x * w[3:4, cs]
        for k in range(CONV_WIDTH - 1):
            y = y + xe[HALO - 3 + k:HALO - 3 + k + tb] * w[k:k + 1, cs]
        halo_ref[0:HALO, cs] = x[tb - HALO:tb]
        if bias is not None:
            y = y + bias[:, cs]
        slabs.append(_silu(y))
        yield
    return jnp.concatenate(slabs, axis=1)


INPROJ_SUB = 256


def _inproj_kernel(n_side, relay, j_sg, j_bg, st_ref, x_ref, g_ref, w_ref, *rest):
    n_in = n_side + (2 if relay else 0)
    n_out = 3 + n_side + (1 if relay else 0)
    side_in = rest[:n_side]
    o_ref, sg_ref, bg_ref = rest[n_in:n_in + 3]
    side_out = rest[n_in + 3:n_in + 3 + n_side]
    h_ref = rest[n_in + n_out]
    j = pl.program_id(1)

    @pl.when(j == 0)
    def _():
        h_ref[...] = _rms(x_ref[...], g_ref[...]).astype(BF16)

    def emit(dst_ref, epilogue):
        for c0 in range(0, w_ref.shape[0], INPROJ_SUB):
            acc = lax.dot_general(h_ref[...], w_ref[c0:c0 + INPROJ_SUB, :], (((1,), (1,)), ((), ())),
                                  preferred_element_type=F32)
            dst_ref[:, c0:c0 + INPROJ_SUB] = epilogue(acc)

    @pl.when(j < j_sg)
    def _():
        emit(o_ref, lambda a: a)

    @pl.when((j >= j_sg) & (j < j_bg))
    def _():
        emit(sg_ref, lambda a: _silu(a).astype(BF16))

    @pl.when(j >= j_bg)
    def _():
        emit(bg_ref, lambda a: _sigmoid(a).astype(BF16))

    for s_ref, d_ref in zip(side_in, side_out):
        d_ref[...] = s_ref[...].astype(BF16)

    if relay:
        main_ref, small_ref = rest[n_side:n_side + 2]
        relay_ref = rest[n_in + 3 + n_side]
        step = pl.program_id(0) * pl.num_programs(1) + j
        rows = relay_ref.shape[0]
        in_small = (step >= _SMALL_COL // rows) & (step < (_SMALL_COL + _SMALL_SLOT) // rows)

        @pl.when(jnp.logical_not(in_small))
        def _():
            relay_ref[...] = main_ref[...].astype(BF16)

        @pl.when(in_small)
        def _():
            relay_ref[...] = small_ref[...].astype(BF16)


def _inproj(x, gain, wt16, li, side, relay=None, *, tm=1024, tn=1024):
    T, D = x.shape
    N = wt16.shape[1]
    ni, nj = T // tm, N // tn
    steps = ni * nj
    j_sg, j_bg = PROJ_W // tn, (PROJ_W + SG_W) // tn
    assert PROJ_W % tn == 0 and SG_W % tn == 0 and N == IN_PAD
    bf16_rows = 16
    side_layer, side_ws = side
    in_specs = [pl.BlockSpec((tm, D), lambda i, j, st: (i, 0)),
                pl.BlockSpec((1, D), lambda i, j, st: (0, 0)),
                pl.BlockSpec((None, tn, D), lambda i, j, st: (li, j, 0))]
    out_specs = [pl.BlockSpec((tm, tn), lambda i, j, st: (i, jnp.minimum(j, j_sg - 1))),
                 pl.BlockSpec((tm, tn), lambda i, j, st: (i, jnp.clip(j - j_sg, 0, j_bg - j_sg - 1))),
                 pl.BlockSpec((tm, tn), lambda i, j, st: (i, jnp.maximum(j - j_bg, 0)))]
    out_shape = [jax.ShapeDtypeStruct((T, PROJ_W), F32), jax.ShapeDtypeStruct((T, SG_W), BF16),
                 jax.ShapeDtypeStruct((T, BG_W), BF16)]
    for w in side_ws:
        _, R, C = w.shape
        rps = -(-R // (steps * bf16_rows)) * bf16_rows
        assert R % rps == 0 and rps % bf16_rows == 0 and R // rps <= steps
        last = R // rps - 1
        in_specs.append(pl.BlockSpec(
            (None, rps, C), lambda i, j, st, last=last: (side_layer, jnp.minimum(i * nj + j, last), 0)))
        out_specs.append(pl.BlockSpec(
            (None, rps, C), lambda i, j, st, last=last: (0, jnp.minimum(i * nj + j, last), 0)))
        out_shape.append(jax.ShapeDtypeStruct((1, R, C), BF16))
    rows = IN_PAD // steps
    operands = [x, gain, wt16, *side_ws]
    if relay is not None:
        nl, wt, small = relay
        assert rows * steps == IN_PAD and rows % bf16_rows == 0 and _SMALL_SLOT % rows == 0
        first_small = _SMALL_COL // rows
        in_specs.append(pl.BlockSpec(
            (None, pl.Element(rows), pl.Element(D)),
            lambda i, j, st: (nl, pl.multiple_of(st[i * nj + j], 16), 0)))
        in_specs.append(pl.BlockSpec(
            (None, rows, D),
            lambda i, j, st: (nl, jnp.clip(i * nj + j - first_small, 0, _SMALL_SLOT // rows - 1), 0)))
        out_specs.append(pl.BlockSpec((None, rows, D), lambda i, j, st: (0, i * nj + j, 0)))
        out_shape.append(jax.ShapeDtypeStruct((1, IN_PAD, D), BF16))
        operands += [wt, small]
    grid_spec = pltpu.PrefetchScalarGridSpec(
        num_scalar_prefetch=1,
        grid=(ni, nj),
        in_specs=in_specs,
        out_specs=out_specs,
        scratch_shapes=[pltpu.VMEM((tm, D), BF16)],
    )
    outs = pl.pallas_call(
        functools.partial(_inproj_kernel, len(side_ws), relay is not None, j_sg, j_bg),
        grid_spec=grid_spec,
        out_shape=out_shape,
        compiler_params=_cparams(("arbitrary", "arbitrary"), VMEM_LIMIT_INPROJ),
        name="inproj",
    )(jnp.asarray(_relayout_row_starts(rows)), *operands)
    n_side = len(side_ws)
    return (outs[0], outs[1], outs[2], outs[3:3 + n_side],
            outs[3 + n_side] if relay is not None else None)


def _unit_lower_inverse_many(ms):
    n = ms[0].shape[0]
    idx = range(len(ms))
    r = _iota((n, n), 0)
    c = _iota((n, n), 1)
    same = (r // 16) == (c // 16)
    eye = (r == c).astype(F32)
    nd = [jnp.where(same, -m, 0.0) for m in ms]
    off = [jnp.where(same, 0.0, m) for m in ms]
    z = [eye + x for x in nd]
    p = [_mm(x, x) for x in nd]
    yield
    for _ in range(2):
        zp = [_mm(jnp.concatenate([z[i], p[i]], axis=0), p[i]) for i in idx]
        z = [z[i] + zp[i][:n] for i in idx]
        p = [zp[i][n:] for i in idx]
        yield
    xd = [z[i] + _mm(z[i], p[i]) for i in idx]
    yield
    n2 = [-_mm(xd[i], off[i]) for i in idx]
    yield
    z = [eye + x for x in n2]
    n4 = [_mm(n2[i], n2[i]) for i in idx]
    yield
    z = [z[i] + _mm(z[i], n4[i]) for i in idx]
    yield
    return [_mm(z[i], xd[i]) for i in idx]


def _dn_stages(q_ref, k_ref, v_ref, gate_ref, sm_ref, cw_ref, par_ref, nw_ref, o_ref,
               hq_ref, hk_ref, hv_ref, s_ref):
    C = CHUNK
    nc = q_ref.shape[0] // C

    @pl.when(pl.program_id(0) == 0)
    def _():
        hq_ref[...] = jnp.zeros_like(hq_ref)
        hk_ref[...] = jnp.zeros_like(hk_ref)
        hv_ref[...] = jnp.zeros_like(hv_ref)
        s_ref[...] = jnp.zeros_like(s_ref)

    cw = cw_ref[...]
    qc = yield from _conv_silu(q_ref, hq_ref, cw[:, 0:MIX_WIDTH], None)
    kc = yield from _conv_silu(k_ref, hk_ref, cw[:, MIX_WIDTH:2 * MIX_WIDTH], None)
    vc = yield from _conv_silu(v_ref, hv_ref, cw[:, 2 * MIX_WIDTH:3 * MIX_WIDTH], None)

    sm = sm_ref[...]
    lane = _iota((1, LANES), 1)
    a_lanes = (lane >= _SMALL_LANE[_DN_A]) & (lane < _SMALL_LANE[_DN_A] + DN_HEADS)
    a_neg = jnp.where(a_lanes, -jnp.exp(par_ref[0:1, :]), 0.0)
    beta_t = _sigmoid(sm)
    g_t = a_neg * _softplus(sm + par_ref[1:2, :])

    r = _iota((C, C), 0)
    c = _iota((C, C), 1)
    causal = r >= c
    strict = r > c
    scale = DN_HEAD_DIM ** -0.5
    nw = nw_ref[...]
    tril = _tril_ones(C)

    H = range(DN_HEADS)
    CH = [(ci, h) for ci in range(nc) for h in H]
    rows = [slice(ci * C, (ci + 1) * C) for ci in range(nc)]
    sls = [slice(h * DN_HEAD_DIM, (h + 1) * DN_HEAD_DIM) for h in H]
    bl = [_SMALL_LANE[_DN_B] + h for h in H]
    gl = [_SMALL_LANE[_DN_A] + h for h in H]

    gc_t = [_sel_left(tril, g_t[rows[ci]]) for ci in range(nc)]
    gc_rows = [jnp.concatenate([g, g], axis=0).T for g in gc_t]
    yield
    qn = [qc[:, sls[h]] for h in H]
    kn = [kc[:, sls[h]] for h in H]
    qn = [x * lax.rsqrt(jnp.sum(x * x, axis=-1, keepdims=True) + NORM_EPS) * scale for x in qn]
    yield
    kn = [x * lax.rsqrt(jnp.sum(x * x, axis=-1, keepdims=True) + NORM_EPS) for x in kn]
    yield
    qh = {(ci, h): qn[h][rows[ci]] for ci, h in CH}
    kh = {(ci, h): kn[h][rows[ci]] for ci, h in CH}
    vh = {(ci, h): vc[rows[ci], sls[h]] for ci, h in CH}
    beta = {(ci, h): beta_t[rows[ci], bl[h]:bl[h] + 1] for ci, h in CH}
    gcol = {(ci, h): gc_t[ci][:, gl[h]:gl[h] + 1] for ci, h in CH}
    grow = {(ci, h): gc_rows[ci][gl[h]:gl[h] + 1, :C] for ci, h in CH}
    glast = {k_: gcol[k_][C - 1:C, :] for k_ in CH}
    decay = {k_: jnp.where(causal, jnp.exp(gcol[k_] - grow[k_]), 0.0) for k_ in CH}
    eg = {k_: jnp.exp(gcol[k_]) for k_ in CH}
    kb = {k_: kh[k_] * beta[k_] for k_ in CH}
    yield
    kq = {k_: _mm_nt(jnp.concatenate([kb[k_], qh[k_]], axis=0), kh[k_]) for k_ in CH}
    yield
    m = [jnp.where(strict, kq[k_][:C] * decay[k_], 0.0) for k_ in CH]
    attn = {k_: kq[k_][C:] * decay[k_] for k_ in CH}
    yield
    ainv = dict(zip(CH, (yield from _unit_lower_inverse_many(m))))
    uw = {k_: _mm(ainv[k_], jnp.concatenate([vh[k_] * beta[k_], kb[k_] * eg[k_]], axis=1)) for k_ in CH}
    yield
    wq = {k_: jnp.concatenate([uw[k_][:, DN_HEAD_DIM:], qh[k_] * eg[k_]], axis=0) for k_ in CH}
    kd = {k_: kh[k_] * jnp.exp(glast[k_] - gcol[k_]) for k_ in CH}
    egl = {k_: jnp.exp(glast[k_]) for k_ in CH}
    yield

    s_cur = [s_ref[h] for h in H]
    out_rows = []
    for ci in range(nc):
        ws = [_mm(wq[ci, h], s_cur[h]) for h in H]
        yield
        v_new = [uw[ci, h][:, :DN_HEAD_DIM] - ws[h][:C] for h in H]
        o = [ws[h][C:] + _mm(attn[ci, h], v_new[h]) for h in H]
        yield
        s_cur = [s_cur[h] * egl[ci, h] + _mm_tn(kd[ci, h], v_new[h]) for h in H]
        yield
        out_rows.append(jnp.concatenate(
            [(_rms(o[h], nw) * gate_ref[rows[ci], sls[h]].astype(F32)).astype(o_ref.dtype) for h in H],
            axis=1))
        yield
    for h in H:
        s_ref[h] = s_cur[h]
    o_ref[...] = jnp.concatenate(out_rows, axis=0)


def _ssd_stages(z_ref, x_ref, bc_ref, sm_ref, cw_ref, cb_ref, par_ref, dsk_ref, nw_ref, o_ref,
                hx_ref, hbc_ref, st_ref):
    C = CHUNK
    TB = x_ref.shape[0]
    nc = TB // C
    P = SSM_HEAD_DIM
    GW = MIX_WIDTH // SSM_GROUPS
    NS = SSM_GROUPS * SSM_STATE

    @pl.when(pl.program_id(0) == 0)
    def _():
        hx_ref[...] = jnp.zeros_like(hx_ref)
        hbc_ref[...] = jnp.zeros_like(hbc_ref)
        st_ref[...] = jnp.zeros_like(st_ref)

    cw = cw_ref[...]
    cb = cb_ref[...]
    xc = yield from _conv_silu(x_ref, hx_ref, cw[:, :MIX_WIDTH], cb[:, :MIX_WIDTH])
    bcc = yield from _conv_silu(bc_ref, hbc_ref, cw[:, MIX_WIDTH:], cb[:, MIX_WIDTH:])

    sm = sm_ref[...]
    lane = _iota((1, LANES), 1)
    l0 = _SMALL_LANE[_S_DT]
    dt_lanes = (lane >= l0) & (lane < l0 + SSM_HEADS)
    a_neg = jnp.where(dt_lanes, -jnp.exp(par_ref[0:1, :]), 0.0)
    dt_t = jnp.where(dt_lanes, _softplus(sm + par_ref[1:2, :]), 0.0)
    tr = _iota((TB, TB), 0)
    tc = _iota((TB, TB), 1)
    blocktril = ((tr >= tc) & ((tr // C) == (tc // C))).astype(F32)
    acs_t = _sel_left(blocktril, dt_t * a_neg)
    yield
    er = _iota((LANES, MIX_WIDTH), 0)
    ec = _iota((LANES, MIX_WIDTH), 1)
    expand = ((er - l0) == (ec // P)).astype(F32)
    both = _sel_right(jnp.concatenate([acs_t, dt_t], axis=0), expand)
    acs_all = both[:TB]
    dt_e = both[TB:]
    rr = _iota((C, MIX_WIDTH), 0)
    cc = _iota((C, MIX_WIDTH), 1) % P
    lane128 = _iota((C, LANES), 1)
    xdt_all = xc * dt_e
    e_acs_all = jnp.exp(acs_all)
    yield

    rows = [slice(ci * C, (ci + 1) * C) for ci in range(nc)]
    CI = range(nc)
    G = range(SSM_GROUPS)
    gs = [slice(g * GW, (g + 1) * GW) for g in G]
    acs = [acs_all[rows[ci]] for ci in CI]
    xdt = [xdt_all[rows[ci]] for ci in CI]
    row_acs = [jnp.sum(jnp.where(rr == cc, acs[ci], 0.0), axis=0, keepdims=True) for ci in CI]
    last = [acs[ci][C - 1:C, :] for ci in CI]
    lmat = [jnp.where(rr >= cc, jnp.exp(acs[ci] - row_acs[ci]), 0.0) for ci in CI]
    xw = [xdt[ci] * jnp.exp(last[ci] - acs[ci]) for ci in CI]
    e_last = [jnp.exp(last[ci]) for ci in CI]
    yield
    bg = {(ci, g): bcc[rows[ci], g * SSM_STATE:(g + 1) * SSM_STATE] for ci in CI for g in G}
    cg = {(ci, g): bcc[rows[ci], NS + g * SSM_STATE:NS + (g + 1) * SSM_STATE] for ci in CI for g in G}
    cb2 = {k_: _mm_nt(cg[k_], jnp.concatenate([bg[k_], bg[k_]], axis=0)) for k_ in bg}
    upd = {(ci, g): _mm_tn(bg[ci, g], xw[ci][:, gs[g]]) for ci in CI for g in G}
    yield
    ydiag = {}
    for ci in CI:
        for g in G:
            parts = []
            for pidx in range(GW // LANES):
                col = g * GW + pidx * LANES
                wmat = cb2[ci, g] * lmat[ci][:, col:col + LANES]
                xp = xdt[ci][:, col:col + LANES]
                bd = jnp.concatenate([jnp.where(lane128 < P, xp, 0.0),
                                      jnp.where(lane128 >= P, xp, 0.0)], axis=0)
                parts.append(_mm(wmat, bd))
            ydiag[ci, g] = jnp.concatenate(parts, axis=1)
            yield

    st_cur = [st_ref[g] for g in G]
    y_rows = []
    for ci in CI:
        y_off = [_mm(cg[ci, g], st_cur[g]) * e_acs_all[rows[ci], gs[g]] for g in G]
        st_cur = [st_cur[g] * e_last[ci][:, gs[g]] + upd[ci, g] for g in G]
        y_rows.append(jnp.concatenate([ydiag[ci, g] + y_off[g] for g in G], axis=1))
        yield
    for g in G:
        st_ref[g] = st_cur[g]
    y = jnp.concatenate(y_rows, axis=0) + dsk_ref[...] * xc
    y = y * z_ref[...].astype(F32)
    yield
    nw = nw_ref[...]
    for g in G:
        o_ref[:, gs[g]] = _rms(y[:, gs[g]], nw[:, gs[g]]).astype(o_ref.dtype)


_GLA_LEVELS = (32, 16, 8, 4, 2, 1)


def _gla_cumsum_matrix():
    i = np.arange(CHUNK)[:, None]
    t = np.arange(CHUNK)[None, :]
    mats = [(t <= i), (t > i)]
    for b in _GLA_LEVELS[:-1]:
        same = (i // b) == (t // b)
        upper = ((i // b) % 2) == 1
        mats.append(np.where(upper, same & (t <= i), same & (t > i)))
    return np.concatenate(mats, axis=0).astype(np.float32)


def _gla_stages(q_ref, k_ref, v_ref, og_ref, sm_ref, w2_ref, b2_ref, cm_ref, nw_ref, o_ref, st_ref):
    C = CHUNK

    @pl.when(pl.program_id(0) == 0)
    def _():
        st_ref[...] = jnp.zeros_like(st_ref)

    nc = q_ref.shape[0] // C
    x = _mm(sm_ref[...], w2_ref[...]) + b2_ref[...]
    gk_all = -_softplus(-x) * (1.0 / GLA_GATE_TEMP)
    r = _iota((C, C), 0)
    c = _iota((C, C), 1)
    rcol = _iota((C, 1), 0)
    scale = GLA_K_DIM ** -0.5
    nw = nw_ref[...]
    cm = cm_ref[...]
    q_blk = q_ref[...] * scale
    k_blk = k_ref[...]

    H = range(GLA_HEADS)
    CI = range(nc)
    CH = [(ci, h) for ci in CI for h in H]
    rows = [slice(ci * C, (ci + 1) * C) for ci in CI]
    ks = [slice(h * GLA_K_DIM, (h + 1) * GLA_K_DIM) for h in H]
    vs = [slice(h * GLA_V_DIM, (h + 1) * GLA_V_DIM) for h in H]
    gk = [gk_all[rows[ci]] for ci in CI]
    yield
    cs = [_sel_left(cm, gk[ci]) for ci in CI]
    yield
    q_all = [q_blk[rows[ci]] for ci in CI]
    k_all = [k_blk[rows[ci]] for ci in CI]
    v = {(ci, h): v_ref[rows[ci], vs[h]] for ci, h in CH}
    attn = {(ci, h): jnp.where(r == c, _mm_nt(q_all[ci][:, ks[h]], k_all[ci][:, ks[h]]), 0.0)
            for ci, h in CH}
    yield
    for lvl, b in enumerate(_GLA_LEVELS):
        upper = ((rcol // b) % 2) == 1
        pair = ((r // (2 * b)) == (c // (2 * b))) & (((r // b) % 2) == 1) & (((c // b) % 2) == 0)
        if b > 1:
            base = C * (lvl + 2)
            fac = [jnp.exp(cs[ci][base:base + C]) for ci in CI]
        else:
            fac = [jnp.exp(jnp.where(upper, gk[ci], 0.0)) for ci in CI]
        qt = [jnp.where(upper, q_all[ci] * fac[ci], 0.0) for ci in CI]
        kt = [jnp.where(upper, 0.0, k_all[ci] * fac[ci]) for ci in CI]
        part = {(ci, h): _mm_nt(qt[ci][:, ks[h]], kt[ci][:, ks[h]]) for ci, h in CH}
        attn = {k_: attn[k_] + jnp.where(pair, part[k_], 0.0) for k_ in CH}
        yield
    qg = [q_all[ci] * jnp.exp(cs[ci][0:C]) for ci in CI]
    kd = [k_all[ci] * jnp.exp(cs[ci][C:2 * C]) for ci in CI]
    dec = [jnp.exp(cs[ci][C - 1:C]) for ci in CI]
    yield
    intra = {k_: _mm(attn[k_], v[k_]) for k_ in CH}
    yield
    upd = {(ci, h): _mm_tn(v[ci, h], kd[ci][:, ks[h]]) for ci, h in CH}
    yield

    st_cur = [st_ref[h] for h in H]
    out_rows = []
    for ci in CI:
        o = [_mm_nt(qg[ci][:, ks[h]], st_cur[h]) + intra[ci, h] for h in H]
        st_cur = [st_cur[h] * dec[ci][:, ks[h]] + upd[ci, h] for h in H]
        yield
        out_rows.append(jnp.concatenate(
            [(_rms(o[h], nw) * og_ref[rows[ci], vs[h]].astype(F32)).astype(o_ref.dtype) for h in H], axis=1))
        yield
    for h in H:
        st_ref[h] = st_cur[h]
    o_ref[...] = jnp.concatenate(out_rows, axis=0)


MIXER_SCHEDULE = "ddsg" * 24


def _mixers_kernel(dq, dk, dv, dg, sm, sz, sx, sbc, gq, gk, gv, go,
                   dn_cw, dn_par, dn_nw, s_cw, s_cb, s_par, s_dsk, s_nw, g_w2, g_b2, g_cm, g_nw,
                   o_dn, o_ssd, o_gla,
                   hq, hk, hv, dn_s, hx, hbc, ssd_st, gla_st):
    gens = {
        "d": _dn_stages(dq, dk, dv, dg, sm, dn_cw, dn_par, dn_nw, o_dn, hq, hk, hv, dn_s),
        "s": _ssd_stages(sz, sx, sbc, sm, s_cw, s_cb, s_par, s_dsk, s_nw, o_ssd, hx, hbc, ssd_st),
        "g": _gla_stages(gq, gk, gv, go, sm, g_w2, g_b2, g_cm, g_nw, o_gla, gla_st),
    }
    live = set(gens)

    def step(name):
        if name in live:
            try:
                next(gens[name])
            except StopIteration:
                live.discard(name)

    for name in MIXER_SCHEDULE:
        step(name)
    while live:
        for name in "dsg":
            step(name)


def _mixers(proj, sgate, dn_conv_w, dn_a_log, dn_dt_bias, dn_norm,
            ssm_conv_w, ssm_conv_b, ssm_dt_bias, ssm_a_log, ssm_d, ssm_norm,
            gla_w2, gla_b2, gla_norm, *, nc=MIXER_CHUNKS_PER_STEP):
    T = proj.shape[0]
    tb = nc * CHUNK
    BCW = 2 * SSM_GROUPS * SSM_STATE
    cw_total = MIX_WIDTH + BCW

    def lanes_par(lane0, n, row0, row1):
        par = jnp.zeros((8, LANES), F32)
        return par.at[0, lane0:lane0 + n].set(row0).at[1, lane0:lane0 + n].set(row1)

    dn_par = lanes_par(_SMALL_LANE[_DN_A], DN_HEADS, dn_a_log, dn_dt_bias)
    s_par = lanes_par(_SMALL_LANE[_S_DT], SSM_HEADS, ssm_a_log, ssm_dt_bias)
    dsk = jnp.repeat(ssm_d, SSM_HEAD_DIM).reshape(1, MIX_WIDTH)
    l0 = _SMALL_LANE[_G_LR]
    w2p = jnp.zeros((LANES, GLA_K_WIDTH), F32).at[l0:l0 + GLA_GATE_RANK].set(gla_w2)
    cm = jnp.asarray(_gla_cumsum_matrix(), BF16)

    blk = lambda seg, w: pl.BlockSpec((tb, w), lambda i, b=_COL[seg] // w: (i, b))
    gate = lambda seg: pl.BlockSpec((tb, MIX_WIDTH),
                                    lambda i, b=(_COL[seg] - PROJ_W) // MIX_WIDTH: (i, b))
    const = lambda shape: pl.BlockSpec(shape, lambda i: (0,) * len(shape))
    out_spec = pl.BlockSpec((tb, MIX_WIDTH), lambda i: (i, 0))
    out_shape = jax.ShapeDtypeStruct((T, MIX_WIDTH), BF16)
    return pl.pallas_call(
        _mixers_kernel,
        grid=(T // tb,),
        in_specs=[blk(_DN_Q, MIX_WIDTH), blk(_DN_K, MIX_WIDTH), blk(_DN_V, MIX_WIDTH),
                  gate(_DN_G),
                  pl.BlockSpec((tb, LANES), lambda i: (i, _SMALL_COL // LANES)),
                  gate(_S_Z), blk(_S_X, MIX_WIDTH), blk(_S_B, BCW),
                  blk(_G_Q, GLA_K_WIDTH), blk(_G_K, GLA_K_WIDTH), blk(_G_V, MIX_WIDTH),
                  gate(_G_O),
                  const((CONV_WIDTH, 3 * MIX_WIDTH)), const((8, LANES)), const((1, DN_HEAD_DIM)),
                  const((CONV_WIDTH, cw_total)), const((1, cw_total)), const((8, LANES)),
                  const((1, MIX_WIDTH)), const((1, MIX_WIDTH)),
                  const((LANES, GLA_K_WIDTH)), const((1, GLA_K_WIDTH)), const(cm.shape),
                  const((1, GLA_V_DIM))],
        out_specs=[out_spec, out_spec, out_spec],
        out_shape=[out_shape, out_shape, out_shape],
        scratch_shapes=[pltpu.VMEM((HALO, MIX_WIDTH), F32)] * 3
        + [pltpu.VMEM((DN_HEADS, DN_HEAD_DIM, DN_HEAD_DIM), F32),
           pltpu.VMEM((HALO, MIX_WIDTH), F32),
           pltpu.VMEM((HALO, BCW), F32),
           pltpu.VMEM((SSM_GROUPS, SSM_STATE, MIX_WIDTH // SSM_GROUPS), F32),
           pltpu.VMEM((GLA_HEADS, GLA_V_DIM, GLA_K_DIM), F32)],
        compiler_params=_cparams(("arbitrary",)),
        name="mixers",
    )(proj, proj, proj, sgate, proj, sgate, proj, proj, proj, proj, proj, sgate,
      dn_conv_w, dn_par, dn_norm.reshape(1, DN_HEAD_DIM),
      ssm_conv_w, ssm_conv_b.reshape(1, cw_total), s_par, dsk, ssm_norm.reshape(1, MIX_WIDTH),
      w2p, gla_b2.reshape(1, GLA_K_WIDTH), cm, gla_norm.reshape(1, GLA_V_DIM))


def _merge_kernel(ya_ref, yb_ref, yc_ref, ga_ref, gb_ref, gc_ref, w_ref, o_ref):
    acc = ga_ref[...].astype(F32) * jnp.dot(ya_ref[...], w_ref[0], preferred_element_type=F32)
    acc += gb_ref[...].astype(F32) * jnp.dot(yb_ref[...], w_ref[1], preferred_element_type=F32)
    acc += gc_ref[...].astype(F32) * jnp.dot(yc_ref[...], w_ref[2], preferred_element_type=F32)
    o_ref[...] = acc.astype(o_ref.dtype)


def _merge(y_dn, y_ssm, y_gla, bgate, w_branch, li, *, tm=1024, tn=1024):
    T = bgate.shape[0]
    ysp = pl.BlockSpec((tm, MIX_WIDTH), lambda i, j: (i, 0))
    gate = lambda b: pl.BlockSpec((tm, tn), lambda i, j, base=b * D_MODEL // tn: (i, base + j))
    return pl.pallas_call(
        _merge_kernel,
        grid=(T // tm, D_MODEL // tn),
        in_specs=[ysp, ysp, ysp, gate(0), gate(1), gate(2),
                  pl.BlockSpec((None, N_BRANCH, MIX_WIDTH, tn), lambda i, j: (li, 0, 0, j))],
        out_specs=pl.BlockSpec((tm, tn), lambda i, j: (i, j)),
        out_shape=jax.ShapeDtypeStruct((T, D_MODEL), BF16),
        compiler_params=_cparams(("parallel", "arbitrary")),
        name="merge",
    )(y_dn, y_ssm, y_gla, bgate, bgate, bgate, w_branch)


def _outproj_kernel(m_ref, x_ref, w_ref, g_ref, o_ref):
    mixed = jnp.dot(m_ref[...], w_ref[...], preferred_element_type=F32)
    o_ref[...] = x_ref[...] + _rms(mixed, g_ref[...])


def _outproj(mixed_pre, x, w_out, li, gain, *, tm=512):
    T, D = x.shape
    return pl.pallas_call(
        _outproj_kernel,
        grid=(T // tm,),
        in_specs=[pl.BlockSpec((tm, D), lambda i: (i, 0)),
                  pl.BlockSpec((tm, D), lambda i: (i, 0)),
                  pl.BlockSpec((None, D, D), lambda i: (li, 0, 0)),
                  pl.BlockSpec((1, D), lambda i: (0, 0))],
        out_specs=pl.BlockSpec((tm, D), lambda i: (i, 0)),
        out_shape=jax.ShapeDtypeStruct((T, D), F32),
        compiler_params=_cparams(("parallel",)),
        name="outproj",
    )(mixed_pre, x, w_out, gain)


def _mlp_kernel(x_ref, gpre_ref, wu_ref, wd_ref, gpost_ref, o_ref, h_ref, acc_ref):
    f = pl.program_id(1)

    @pl.when(f == 0)
    def _():
        h_ref[...] = _rms(x_ref[...], gpre_ref[...]).astype(BF16)
        acc_ref[...] = jnp.zeros_like(acc_ref)

    up = jnp.dot(h_ref[...], wu_ref[...], preferred_element_type=F32)
    act = jnp.square(jnp.maximum(up, 0.0)).astype(BF16)
    acc_ref[...] += jnp.dot(act, wd_ref[...], preferred_element_type=F32)

    @pl.when(f == pl.num_programs(1) - 1)
    def _():
        o_ref[...] = x_ref[...] + _rms(acc_ref[...], gpost_ref[...])


def _mlp(x, gpre, w_up, w_down, li, gpost, *, tm=512, tf=1024):
    T, D = x.shape
    F = w_up.shape[2]
    return pl.pallas_call(
        _mlp_kernel,
        grid=(T // tm, F // tf),
        in_specs=[pl.BlockSpec((tm, D), lambda i, f: (i, 0)),
                  pl.BlockSpec((1, D), lambda i, f: (0, 0)),
                  pl.BlockSpec((None, D, tf), lambda i, f: (li, 0, f)),
                  pl.BlockSpec((None, tf, D), lambda i, f: (li, f, 0)),
                  pl.BlockSpec((1, D), lambda i, f: (0, 0))],
        out_specs=pl.BlockSpec((tm, D), lambda i, f: (i, 0)),
        out_shape=jax.ShapeDtypeStruct((T, D), F32),
        scratch_shapes=[pltpu.VMEM((tm, D), BF16), pltpu.VMEM((tm, D), F32)],
        compiler_params=_cparams(("parallel", "arbitrary")),
        name="mlp",
    )(x, gpre, w_up, w_down, gpost)


def _ple_kernel(x_ref, p_ref, gpre_ref, wg_ref, wp_ref, gpost_ref, o_ref):
    x = x_ref[...]
    h = _rms(x, gpre_ref[...]).astype(BF16)
    gate = _sigmoid(jnp.dot(h, wg_ref[...], preferred_element_type=F32))
    e = jnp.dot(p_ref[...].astype(BF16), wp_ref[...], preferred_element_type=F32) * gate
    o_ref[...] = x + _rms(e, gpost_ref[...])


def _ple(x, p, b, gpre, w_gate, w_proj, li, gpost, p_layer, *, tm=512):
    T, D = x.shape
    return pl.pallas_call(
        _ple_kernel,
        grid=(T // tm,),
        in_specs=[pl.BlockSpec((tm, D), lambda i: (i, 0)),
                  pl.BlockSpec((None, None, tm, PLE_DIM), lambda i: (p_layer, b, i, 0)),
                  pl.BlockSpec((1, D), lambda i: (0, 0)),
                  pl.BlockSpec((None, D, D), lambda i: (li, 0, 0)),
                  pl.BlockSpec((None, PLE_DIM, D), lambda i: (li, 0, 0)),
                  pl.BlockSpec((1, D), lambda i: (0, 0))],
        out_specs=pl.BlockSpec((tm, D), lambda i: (i, 0)),
        out_shape=jax.ShapeDtypeStruct((T, D), F32),
        compiler_params=_cparams(("parallel",)),
        name="ple",
    )(x, p, gpre, w_gate, w_proj, gpost)


RELAY_TN = _SMALL_SLOT
_SMALL_TILE = _SMALL_COL // RELAY_TN


def _relayout_row_starts(rows=RELAY_TN):
    src = np.full((IN_PAD,), -1, np.int64)
    for s in _LAYOUT:
        if s != _SMALL:
            src[_COL[s]:_COL[s] + IN_SPLITS[s]] = np.arange(_IN_OFF[s], _IN_OFF[s + 1])
    starts = src[::rows].copy()
    assert _SMALL_COL % rows == 0 and _SMALL_SLOT % rows == 0
    small_tiles = range(_SMALL_COL // rows, (_SMALL_COL + _SMALL_SLOT) // rows)
    for j, s0 in enumerate(starts):
        if j in small_tiles:
            starts[j] = 0
        else:
            assert np.array_equal(src[j * rows:(j + 1) * rows], s0 + np.arange(rows))
            assert s0 % 16 == 0
    return starts.astype(np.int32)


def _relayout_kernel(start_ref, a_ref, small_ref, o_ref):
    j = pl.program_id(1)

    @pl.when(j != _SMALL_TILE)
    def _():
        o_ref[...] = a_ref[...].astype(BF16)

    @pl.when(j == _SMALL_TILE)
    def _():
        o_ref[...] = small_ref[...].astype(BF16)


def _transposed_w_in(w_in):
    wt = jnp.swapaxes(w_in, 1, 2)
    small = jnp.concatenate([wt[:, _IN_OFF[s]:_IN_OFF[s + 1], :] for s in _SMALL_ORDER], axis=1)
    small = jnp.pad(small, ((0, 0), (0, _SMALL_SLOT - small.shape[1]), (0, 0)))
    return wt, small


def _relayout_w_in(wt, small, layer):
    D = wt.shape[2]
    starts = _relayout_row_starts()
    nj = starts.shape[0]
    assert nj * RELAY_TN == IN_PAD
    grid_spec = pltpu.PrefetchScalarGridSpec(
        num_scalar_prefetch=1,
        grid=(1, nj),
        in_specs=[pl.BlockSpec((None, pl.Element(RELAY_TN), pl.Element(D)),
                               lambda l, j, st: (layer, pl.multiple_of(st[j], 16), 0)),
                  pl.BlockSpec((None, RELAY_TN, D), lambda l, j, st: (layer, 0, 0))],
        out_specs=pl.BlockSpec((None, RELAY_TN, D), lambda l, j, st: (0, j, 0)),
    )
    return pl.pallas_call(
        _relayout_kernel,
        grid_spec=grid_spec,
        out_shape=jax.ShapeDtypeStruct((1, IN_PAD, D), BF16),
        compiler_params=_cparams(("arbitrary", "arbitrary")),
        name="relayout_w_in",
    )(jnp.asarray(starts), wt, small)


def kernel(x, p, pre_mix_norm, w_in, dn_conv_w, dn_a_log, dn_dt_bias, dn_norm, ssm_conv_w, ssm_conv_b, ssm_dt_bias, ssm_a_log, ssm_d, ssm_norm, gla_gate_w2, gla_gate_b, gla_norm, w_branch, w_out, post_mix_norm, pre_mlp_norm, w_up, w_down, post_mlp_norm, ple_pre_norm, w_ple_gate, w_ple_proj, ple_post_norm):
    Bsz, T, D = x.shape
    depth = w_in.shape[0]
    row = lambda g: g.reshape(1, -1)
    wt, small = _transposed_w_in(w_in)
    w_in16 = {0: _relayout_w_in(wt, small, 0)}
    side_f32 = [w_branch.reshape(depth, N_BRANCH * MIX_WIDTH, D), w_out, w_up, w_down,
                w_ple_gate, w_ple_proj]
    cast = {}
    outs = []
    for b in range(Bsz):
        xb = x[b]
        for i in range(depth):
            relay = (i + 1, wt, small) if i + 1 < depth and i + 1 not in w_in16 else None
            proj, sgate, bgate, side, nxt = _inproj(xb, row(pre_mix_norm[i]), w_in16[i], 0,
                                                    (i, [] if i in cast else side_f32), relay)
            if relay is not None:
                w_in16[i + 1] = nxt
            cast.setdefault(i, side)
            w_br16, w_out16, w_up16, w_down16, w_pg16, w_pp16 = cast[i]
            w_br16 = w_br16.reshape(1, N_BRANCH, MIX_WIDTH, D)
            y_dn, y_ssm, y_gla = _mixers(
                proj, sgate, dn_conv_w[i], dn_a_log[i], dn_dt_bias[i], dn_norm[i],
                ssm_conv_w[i], ssm_conv_b[i], ssm_dt_bias[i], ssm_a_log[i], ssm_d[i], ssm_norm[i],
                gla_gate_w2[i], gla_gate_b[i], gla_norm[i])
            mixed_pre = _merge(y_dn, y_ssm, y_gla, bgate, w_br16, 0)
            xb = _outproj(mixed_pre, xb, w_out16, 0, row(post_mix_norm[i]))
            xb = _mlp(xb, row(pre_mlp_norm[i]), w_up16, w_down16, 0, row(post_mlp_norm[i]))
            xb = _ple(xb, p, b, row(ple_pre_norm[i]), w_pg16, w_pp16, 0, row(ple_post_norm[i]), i)
        outs.append(xb)
    return jnp.stack(outs, axis=0)
```
